```python
import jax, jax.numpy as jnp
from jax import lax
import numpy as np

D_MODEL = 1024
BATCH = 8
SEQ = 4096
DEPTH = 2

CHUNK = 64
N_EVEN = (DEPTH + 1) // 2
N_ODD = DEPTH // 2

A_HEADS = 4
A_DK = 128
A_DV = 128
A_WIDTH = A_HEADS * A_DV
B_HEADS = 8
B_HD = 64
B_WIDTH = B_HEADS * B_HD
IDX_HEADS = 8
IDX_DIM = 64
TOPK_MAX = 256
Q_BLOCK = 128
EVEN_SIZES = (A_HEADS * A_DK,
              A_HEADS * A_DK,
              A_WIDTH,
              A_WIDTH,
              B_WIDTH,
              B_HD,
              B_HD,
              IDX_HEADS * IDX_DIM,
              IDX_DIM,
              IDX_HEADS)
EVEN_IN = sum(EVEN_SIZES)
EVEN_MIX = A_WIDTH + B_WIDTH
LRU_WIDTH = 1280
LRU_BLOCKS = 10
LRU_BW = LRU_WIDTH // LRU_BLOCKS
LRU_CONV = 4
LRU_C = 8.0
D_FF = 3072
FFN_CONV = 3
EPS = 1e-6

kernel_name = "hgrn2_dsa_rglru_convffn_hybrid"


def rms_norm(x, gain):
    xf = x.astype(jnp.float32)
    y = xf * lax.rsqrt(jnp.mean(xf * xf, axis=-1, keepdims=True) + EPS)
    return (y * gain.astype(jnp.float32)).astype(x.dtype)


def causal_dwconv(x, w, b):
    width, ch = w.shape
    y = lax.conv_general_dilated(x, w[:, None, :].astype(x.dtype), window_strides=(1,),
                                 padding=[(width - 1, 0)],
                                 dimension_numbers=('NWC', 'WIO', 'NWC'),
                                 feature_group_count=ch)
    return y + b.astype(x.dtype)


def split_cols(z, sizes):
    idx = [int(v) for v in np.cumsum(sizes)[:-1]]
    return jnp.split(z, idx, axis=-1)


def hgrn_lower_bound(lb_logits, layer):
    p = jax.nn.softmax(lb_logits.astype(jnp.float32), axis=0)
    return jnp.cumsum(p, axis=0)[layer]


def hgrn2(q, f_raw, i, lb):
    bn, s, _ = q.shape
    nc = s // CHUNK

    def to_chunks(t, d):
        return t.astype(jnp.float32).reshape(bn, nc, CHUNK, A_HEADS, d).transpose(1, 0, 3, 2, 4)

    f = lb + (1.0 - lb) * jax.nn.sigmoid(f_raw.astype(jnp.float32))
    qc = to_chunks(jax.nn.silu(q.astype(jnp.float32)), A_DK)
    gc = to_chunks(jnp.log(f), A_DK)
    kc = to_chunks(1.0 - f, A_DK)
    vc = to_chunks(i, A_DV)
    causal = jnp.tril(jnp.ones((CHUNK, CHUNK), dtype=bool))[:, :, None]

    def step(state, inp):
        qb, kb, vb, gb = inp
        b = jnp.cumsum(gb, axis=2)
        inter = jnp.einsum('bhtk,bhkv->bhtv', qb * jnp.exp(b), state)
        rel = b[:, :, :, None, :] - b[:, :, None, :, :]
        decay = jnp.where(causal, jnp.exp(jnp.where(causal, rel, 0.0)), 0.0)
        scores = jnp.einsum('bhtk,bhtsk,bhsk->bhts', qb, decay, kb)
        intra = jnp.einsum('bhts,bhsv->bhtv', scores, vb)
        b_last = b[:, :, -1:, :]
        new_state = (jnp.exp(b_last[:, :, 0, :])[..., None] * state
                     + jnp.einsum('bhsk,bhsv->bhkv', kb * jnp.exp(b_last - b), vb))
        return new_state, inter + intra

    s0 = jnp.zeros((bn, A_HEADS, A_DK, A_DV), jnp.float32)
    _, o = lax.scan(step, s0, (qc, kc, vc, gc))
    return o.transpose(1, 0, 3, 2, 4).reshape(bn, s, A_HEADS, A_DV)


def dsa_attention(q, k, v, qi, ki, wi, q_gain, k_gain):
    bn, s = q.shape[:2]
    topk = min(TOPK_MAX, s // 4)
    q = rms_norm(q, q_gain)
    k = rms_norm(k, k_gain)
    key_chunk = jnp.arange(s) // CHUNK
    nb = s // Q_BLOCK
    scale = B_HD ** -0.5
    idx_scale = IDX_DIM ** -0.5
    kif = ki.astype(jnp.float32)

    def to_blocks(t):
        return jnp.moveaxis(t.reshape(bn, nb, Q_BLOCK, *t.shape[2:]), 1, 0)

    gather = jax.vmap(lambda tb, ib: tb[ib])

    def one_block(args):
        qb, qib, wib, start = args
        q_chunk = (start + jnp.arange(Q_BLOCK)) // CHUNK
        admissible = key_chunk[None, :] <= q_chunk[:, None]
        dots = jnp.einsum('bqhd,bsd->bqhs', qib.astype(jnp.float32), kif) * idx_scale
        iscore = jnp.einsum('bqh,bqhs->bqs', wib.astype(jnp.float32), jax.nn.relu(dots))
        iscore = jnp.where(admissible[None], iscore, -jnp.inf)
        _, sel = lax.top_k(iscore, topk)
        valid = key_chunk[sel] <= q_chunk[None, :, None]
        kg = gather(k, sel).astype(jnp.float32)
        vg = gather(v, sel).astype(jnp.float32)
        logits = jnp.einsum('bqhd,bqkd->bqhk', qb.astype(jnp.float32), kg) * scale
        logits = jnp.where(valid[:, :, None, :], logits, -jnp.inf)
        p = jax.nn.softmax(logits, axis=-1)
        return jnp.einsum('bqhk,bqkd->bqhd', p, vg).astype(v.dtype)

    starts = jnp.arange(nb) * Q_BLOCK
    o = lax.map(one_block, (to_blocks(q), to_blocks(qi), to_blocks(wi), starts))
    return jnp.moveaxis(o, 0, 1).reshape(bn, s, B_HEADS * B_HD)


def hgrn_dsa_mixer(h, w_in, w_out, lb, a_norm, q_gain, k_gain):
    bn, s, _ = h.shape
    (a_q, a_f, a_i, a_g, b_q, b_k, b_v, ix_q, ix_k, ix_w) = split_cols(h @ w_in, EVEN_SIZES)
    a_o = hgrn2(a_q, a_f, a_i, lb)
    a_o = rms_norm(a_o, a_norm.reshape(A_HEADS, A_DV)).reshape(bn, s, A_WIDTH).astype(h.dtype)
    a_o = a_o * jax.nn.silu(a_g)
    b_o = dsa_attention(b_q.reshape(bn, s, B_HEADS, B_HD), b_k, b_v,
                        ix_q.reshape(bn, s, IDX_HEADS, IDX_DIM), ix_k,
                        ix_w * (IDX_HEADS ** -0.5), q_gain, k_gain)
    return jnp.concatenate([a_o, b_o], axis=-1) @ w_out


def rglru_mixer(h, w_in, conv_w, conv_b, wa, ba, wx, bx, lam, w_out):
    bn, s, _ = h.shape
    y_br, x_br = jnp.split(h @ w_in, 2, axis=-1)
    y_br = jax.nn.gelu(y_br, approximate=True)
    xc = causal_dwconv(x_br, conv_w, conv_b)
    xb = xc.reshape(bn, s, LRU_BLOCKS, LRU_BW).astype(jnp.float32)
    r = jax.nn.sigmoid(jnp.einsum('bsnc,ncd->bsnd', xb, wa.astype(jnp.float32)).reshape(bn, s, LRU_WIDTH)
                       + ba.astype(jnp.float32))
    gi = jax.nn.sigmoid(jnp.einsum('bsnc,ncd->bsnd', xb, wx.astype(jnp.float32)).reshape(bn, s, LRU_WIDTH)
                        + bx.astype(jnp.float32))
    log_a = LRU_C * r * jax.nn.log_sigmoid(lam.astype(jnp.float32))
    a = jnp.exp(log_a)
    u = jnp.sqrt(-jnp.expm1(2.0 * log_a)) * (gi * xc.astype(jnp.float32))

    def combine(left, right):
        a1, b1 = left
        a2, b2 = right
        return a1 * a2, a2 * b1 + b2

    _, hs = lax.associative_scan(combine, (a, u), axis=1)
    return (hs.astype(h.dtype) * y_br) @ w_out


def conv_ffn(h, w_up, conv_w, conv_b, w_down):
    u = causal_dwconv(h @ w_up, conv_w, conv_b)
    gate, val = jnp.split(u, 2, axis=-1)
    return (jax.nn.gelu(gate, approximate=True) * val) @ w_down


def setup_inputs(seed: int = 0) -> dict:
    key = jax.random.key(seed)
    ks = iter(jax.random.split(key, 32))

    def nrm(shape, scale):
        return jax.random.normal(next(ks), shape, jnp.float32) * scale

    def gain(shape):
        return 1.0 + nrm(shape, 0.01)

    a8 = jax.random.uniform(next(ks), (N_ODD, LRU_WIDTH), jnp.float32, 0.81, 0.998)
    p = a8 ** (1.0 / LRU_C)
    lam = jnp.log(p) - jnp.log1p(-p)
    return {
        "x": nrm((BATCH, SEQ, D_MODEL), 1.0),
        "lb_logits": nrm((DEPTH + 1, A_HEADS * A_DK), 0.1),
        "even_norm": gain((N_EVEN, D_MODEL)),
        "even_w_in": nrm((N_EVEN, D_MODEL, EVEN_IN), D_MODEL ** -0.5),
        "even_w_out": nrm((N_EVEN, EVEN_MIX, D_MODEL), EVEN_MIX ** -0.5),
        "a_out_norm": gain((N_EVEN, A_WIDTH)),
        "b_q_norm": gain((N_EVEN, B_HD)),
        "b_k_norm": gain((N_EVEN, B_HD)),
        "odd_norm": gain((N_ODD, D_MODEL)),
        "odd_w_in": nrm((N_ODD, D_MODEL, 2 * LRU_WIDTH), D_MODEL ** -0.5),
        "odd_conv_w": nrm((N_ODD, LRU_CONV, LRU_WIDTH), LRU_CONV ** -0.5),
        "odd_conv_b": nrm((N_ODD, LRU_WIDTH), 0.01),
        "odd_gate_a_w": nrm((N_ODD, LRU_BLOCKS, LRU_BW, LRU_BW), LRU_BW ** -0.5),
        "odd_gate_a_b": nrm((N_ODD, LRU_WIDTH), 0.01),
        "odd_gate_x_w": nrm((N_ODD, LRU_BLOCKS, LRU_BW, LRU_BW), LRU_BW ** -0.5),
        "odd_gate_x_b": nrm((N_ODD, LRU_WIDTH), 0.01),
        "odd_lambda": lam,
        "odd_w_out": nrm((N_ODD, LRU_WIDTH, D_MODEL), LRU_WIDTH ** -0.5),
        "ffn_norm": gain((DEPTH, D_MODEL)),
        "ffn_w_up": nrm((DEPTH, D_MODEL, 2 * D_FF), D_MODEL ** -0.5),
        "ffn_conv_w": nrm((DEPTH, FFN_CONV, 2 * D_FF), FFN_CONV ** -0.5),
        "ffn_conv_b": nrm((DEPTH, 2 * D_FF), 0.01),
        "ffn_w_down": nrm((DEPTH, D_FF, D_MODEL), D_FF ** -0.5),
    }


def reference(x, lb_logits, even_norm, even_w_in, even_w_out, a_out_norm, b_q_norm, b_k_norm,
              odd_norm, odd_w_in, odd_conv_w, odd_conv_b, odd_gate_a_w, odd_gate_a_b,
              odd_gate_x_w, odd_gate_x_b, odd_lambda, odd_w_out,
              ffn_norm, ffn_w_up, ffn_conv_w, ffn_conv_b, ffn_w_down):
    h = x
    for layer in range(DEPTH):
        j = layer // 2
        if layer % 2 == 0:
            lb = hgrn_lower_bound(lb_logits, layer)
            h = h + hgrn_dsa_mixer(rms_norm(h, even_norm[j]), even_w_in[j], even_w_out[j], lb,
                                   a_out_norm[j], b_q_norm[j], b_k_norm[j])
        else:
            h = h + rglru_mixer(rms_norm(h, odd_norm[j]), odd_w_in[j], odd_conv_w[j], odd_conv_b[j],
                                odd_gate_a_w[j], odd_gate_a_b[j], odd_gate_x_w[j], odd_gate_x_b[j],
                                odd_lambda[j], odd_w_out[j])
        h = h + conv_ffn(rms_norm(h, ffn_norm[layer]), ffn_w_up[layer], ffn_conv_w[layer],
                         ffn_conv_b[layer], ffn_w_down[layer])
    return h
```

```python
import functools

import numpy as np
import jax
import jax.numpy as jnp
from jax import lax
from jax.experimental import pallas as pl
from jax.experimental.pallas import tpu as pltpu

F32 = jnp.float32
BF16 = jnp.bfloat16

EPS = 1e-6
CHUNK = 64
A_HEADS = 4
A_D = 128
B_HEADS = 8
B_HD = 64
IDX_HEADS = 8
IDX_DIM = 64
TOPK_MAX = 256
Q_BLOCK = 128
LRU_BLOCKS = 10
LRU_BW = 128
LRU_C = 8.0
LANES = 128
SUBLANES = 8
VMEM_LIMIT = 56 * 1024 * 1024

Z_A = 0
Z_BQ = 2048
Z_IQ = 2560
Z_SM = 3072
Z_W = 3328
SM_W = 256
NEG_BIG = -1e30


def _cparams(sem):
    return pltpu.CompilerParams(dimension_semantics=sem, vmem_limit_bytes=VMEM_LIMIT)


def _rms(x, gain):
    ms = jnp.mean(x * x, axis=-1, keepdims=True)
    return x * lax.rsqrt(ms + EPS) * gain


def _sigmoid(x):
    return 1.0 / (1.0 + jnp.exp(-x))


def _dot(a, b):
    return jnp.dot(a, b, preferred_element_type=F32)


def _dot_nt(a, b):
    return lax.dot_general(a, b, (((1,), (1,)), ((), ())), preferred_element_type=F32)


def _dot_tn(a, b):
    return lax.dot_general(a, b, (((0,), (0,)), ((), ())), preferred_element_type=F32)


def _norm_proj_kernel(x_ref, g_ref, w_ref, o_ref, xn_ref):
    @pl.when(pl.program_id(1) == 0)
    def _():
        xn_ref[...] = _rms(x_ref[...], g_ref[...]).astype(BF16)

    o_ref[...] = _dot(xn_ref[...], w_ref[...]).astype(o_ref.dtype)


def _norm_proj(x2d, gain, w, *, tm, tn, out_dtype):
    m, d = x2d.shape
    n = w.shape[1]
    return pl.pallas_call(
        _norm_proj_kernel,
        grid=(m // tm, n // tn),
        in_specs=[pl.BlockSpec((tm, d), lambda i, j: (i, 0)),
                  pl.BlockSpec((1, d), lambda i, j: (0, 0)),
                  pl.BlockSpec((d, tn), lambda i, j: (0, j))],
        out_specs=pl.BlockSpec((tm, tn), lambda i, j: (i, j)),
        out_shape=jax.ShapeDtypeStruct((m, n), out_dtype),
        scratch_shapes=[pltpu.VMEM((tm, d), BF16)],
        compiler_params=_cparams(("arbitrary", "arbitrary")),
        name="even_in_proj",
    )(x2d, gain.reshape(1, d), w)


def _hgrn_consts(c):
    levels = int(np.log2(c))
    tri = np.tril(np.ones((c, c), np.float32))
    mats = [tri]
    masks = []
    t = np.arange(c)
    for l in range(levels):
        h = c >> (l + 1)
        base = (t // (2 * h)) * (2 * h)
        mid = base + h - 1
        mats.append(tri[mid])
        second = (t % (2 * h)) >= h
        same = base[:, None] == base[None, :]
        masks.append((same & second[:, None] & (~second)[None, :]).astype(np.float32))
    masks.append(np.eye(c, dtype=np.float32))
    return np.concatenate(mats, 0), np.stack(masks, 0), levels


def _hgrn_kernel(layer, c, levels, q_ref, f_ref, i_ref, g_ref, lbl_ref, an_ref, sel_ref, msk_ref,
                 o_ref, st_ref):
    @pl.when(pl.program_id(2) == 0)
    def _():
        st_ref[...] = jnp.zeros_like(st_ref)

    lg = lbl_ref[...]
    e = jnp.exp(lg - jnp.max(lg, axis=0, keepdims=True))
    p = e / jnp.sum(e, axis=0, keepdims=True)
    lb = jnp.sum(p[:layer + 1], axis=0, keepdims=True)
    an = an_ref[...]
    n_chunks = q_ref.shape[0] // c

    def chunk(ci, carry):
        r0 = pl.multiple_of(ci * c, c)
        qr = q_ref[pl.ds(r0, c), :].astype(F32)
        fr = f_ref[pl.ds(r0, c), :].astype(F32)
        v = i_ref[pl.ds(r0, c), :].astype(BF16)
        gr = g_ref[pl.ds(r0, c), :].astype(F32)
        q = qr * _sigmoid(qr)
        f = lb + (1.0 - lb) * _sigmoid(fr)
        g = jnp.log(f)
        k = 1.0 - f
        g1 = g.astype(BF16)
        r1 = g - g1.astype(F32)
        g2 = r1.astype(BF16)
        g3 = (r1 - g2.astype(F32)).astype(BF16)
        rs = _dot(sel_ref[...], jnp.concatenate([g1, g2, g3], axis=1))
        rs = rs[:, :A_D] + rs[:, A_D:2 * A_D] + rs[:, 2 * A_D:]
        b = rs[:c]
        scores = msk_ref[levels] * _dot_nt(q.astype(BF16), k.astype(BF16))
        for l in range(levels):
            x = jnp.exp(-jnp.abs(b - rs[(l + 1) * c:(l + 2) * c]))
            scores = scores + msk_ref[l] * _dot_nt((q * x).astype(BF16), (k * x).astype(BF16))
        st = st_ref[...]
        inter = _dot_nt((q * jnp.exp(b)).astype(BF16), st.astype(BF16))
        o = inter + _dot(scores.astype(BF16), v)
        b_last = b[c - 1:c, :]
        kd = (k * jnp.exp(b_last - b)).astype(BF16)
        st_ref[...] = jnp.exp(b_last) * st + _dot_tn(v, kd)
        o = _rms(o, an) * (gr * _sigmoid(gr))
        o_ref[pl.ds(r0, c), :] = o.astype(o_ref.dtype)
        return carry

    lax.fori_loop(0, n_chunks, chunk, 0)


def _hgrn(z3, lb_logits, a_norm, layer, *, t_rows):
    bsz, s, _ = z3.shape
    sel, msk, levels = _hgrn_consts(CHUNK)
    sel = jnp.asarray(sel, BF16)
    msk = jnp.asarray(msk, F32)
    nl = lb_logits.shape[0]
    col = Z_A // A_D

    def zspec(k):
        return pl.BlockSpec((None, t_rows, A_D), lambda b, h, t: (b, t, col + k * A_HEADS + h))

    return pl.pallas_call(
        functools.partial(_hgrn_kernel, layer, CHUNK, levels),
        grid=(bsz, A_HEADS, s // t_rows),
        in_specs=[zspec(0), zspec(1), zspec(2), zspec(3),
                  pl.BlockSpec((nl, A_D), lambda b, h, t: (0, h)),
                  pl.BlockSpec((1, A_D), lambda b, h, t: (0, h)),
                  pl.BlockSpec(sel.shape, lambda b, h, t: (0, 0)),
                  pl.BlockSpec(msk.shape, lambda b, h, t: (0, 0, 0))],
        out_specs=pl.BlockSpec((None, t_rows, A_D), lambda b, h, t: (b, t, h)),
        out_shape=jax.ShapeDtypeStruct((bsz, s, A_HEADS * A_D), BF16),
        scratch_shapes=[pltpu.VMEM((A_D, A_D), F32)],
        compiler_params=_cparams(("arbitrary", "arbitrary", "arbitrary")),
        name="hgrn2",
    )(z3, z3, z3, z3, lb_logits, a_norm.reshape(1, -1), sel, msk)


INT_MIN = -2 ** 31
KEY_MASK = 0x7FFFFFFF


def _sort_key(x):
    bits = pltpu.bitcast(x, jnp.int32)
    key = bits ^ ((bits >> 31) & KEY_MASK)
    return jnp.where(x == 0.0, 0, key)


def _dsa_kernel(topk, tk, zq_ref, zi_ref, zs_ref, qg_ref, kg_ref, tri_ref, o_ref,
                kit_ref, knt_ref, v_ref, key_ref, qa_ref, qi_ref, wb_ref):
    qb = pl.program_id(1)
    n_tiles_all = kit_ref.shape[0]
    lane = lax.broadcasted_iota(jnp.int32, (1, LANES), 1)
    is_k = lane < B_HD

    @pl.when(qb == 0)
    def _():
        kg = kg_ref[...]

        def prep(t, carry):
            r0 = pl.multiple_of(t * tk, tk)
            rows = zs_ref[pl.ds(r0, tk), :].astype(F32)
            kv = rows[:, :LANES]
            ms = jnp.sum(jnp.where(is_k, kv * kv, 0.0), axis=-1, keepdims=True) * (1.0 / B_HD)
            kvn = kv * jnp.where(is_k, lax.rsqrt(ms + EPS) * kg, 1.0)
            kvt = kvn.T
            knt_ref[t] = kvt[:B_HD].astype(BF16)
            v_ref[t] = rows[:, B_HD:2 * B_HD].astype(BF16)
            kit = rows[:, LANES:].T
            kit_ref[t] = kit[:IDX_DIM].astype(BF16)
            return carry

        lax.fori_loop(0, n_tiles_all, prep, 0)

    q = zq_ref[...].astype(F32)
    qi = zi_ref[...].astype(F32)
    qg = qg_ref[...] * (B_HD ** -0.5)
    for h in range(B_HEADS):
        qh = q[:, h * B_HD:(h + 1) * B_HD]
        qa_ref[h * Q_BLOCK:(h + 1) * Q_BLOCK, :] = _rms(qh, qg).astype(BF16)
    for h in range(IDX_HEADS):
        qi_ref[h * Q_BLOCK:(h + 1) * Q_BLOCK, :] = qi[:, h * IDX_DIM:(h + 1) * IDX_DIM].astype(BF16)
    q0 = pl.multiple_of(qb * Q_BLOCK, Q_BLOCK)
    w = zs_ref[pl.ds(q0, Q_BLOCK), :].astype(F32)[:, 3 * B_HD:3 * B_HD + IDX_HEADS]
    w = w * ((IDX_HEADS ** -0.5) * (IDX_DIM ** -0.5))
    for h in range(IDX_HEADS):
        wb_ref[h] = jnp.broadcast_to(w[:, h:h + 1], (Q_BLOCK, LANES))

    n_tiles = (q0 + Q_BLOCK + tk - 1) // tk
    row = lax.broadcasted_iota(jnp.int32, (Q_BLOCK, 1), 0)
    limit = q0 + jnp.where(row < CHUNK, CHUNK, 2 * CHUNK)
    reps = tk // LANES

    def admissible(t):
        col = t * tk + lax.broadcasted_iota(jnp.int32, (Q_BLOCK, tk), 1)
        return col < limit

    def index_tile(t, carry):
        d = _dot(qi_ref[...], kit_ref[t])
        acc = jnp.zeros((Q_BLOCK, tk), F32)
        for h in range(IDX_HEADS):
            wh = pltpu.repeat(wb_ref[h], reps, axis=1)
            acc = acc + jnp.maximum(d[h * Q_BLOCK:(h + 1) * Q_BLOCK], 0.0) * wh
        acc = jnp.where(admissible(t), acc, -jnp.inf)
        key_ref[t] = _sort_key(acc)
        return carry

    lax.fori_loop(0, n_tiles, index_tile, 0)

    def count_ge(cand):
        def body(t, acc):
            keys = key_ref[t]
            for j in range(reps):
                acc = acc + jnp.where(keys[:, j * LANES:(j + 1) * LANES] >= cand, 1.0, 0.0)
            return acc
        acc = lax.fori_loop(0, n_tiles, body, jnp.zeros((Q_BLOCK, LANES), F32))
        return jnp.sum(acc, axis=-1, keepdims=True)

    def bit_step(i, thr):
        cand = thr + lax.shift_left(jnp.int32(1), 31 - i)
        cnt = count_ge(cand)
        return jnp.where(cnt >= float(topk), cand, thr)

    thr = lax.fori_loop(0, 32, bit_step, jnp.full((Q_BLOCK, LANES), INT_MIN, jnp.int32))
    cnt_gt = count_ge(thr + 1)
    need = float(topk) - cnt_gt
    thr_t = pltpu.repeat(thr, reps, axis=1)

    def attend(t, carry):
        run, m, l, acc = carry
        keys = key_ref[t]
        gt = keys > thr_t
        eq = keys == thr_t
        eqf = jnp.where(eq, 1.0, 0.0)
        rank = _dot(eqf.astype(BF16), tri_ref[...]) + run
        sel = (gt | (eq & (rank < need))) & admissible(t)
        run = run + jnp.sum(eqf, axis=-1, keepdims=True)
        s_all = _dot(qa_ref[...], knt_ref[t])
        vt = v_ref[t]
        m_new, l_new, acc_new = [], [], []
        for h in range(B_HEADS):
            s = jnp.where(sel, s_all[h * Q_BLOCK:(h + 1) * Q_BLOCK], NEG_BIG)
            mh = jnp.maximum(m[h], jnp.max(s, axis=-1, keepdims=True))
            p = jnp.where(sel, jnp.exp(s - mh), 0.0)
            alpha = jnp.exp(m[h] - mh)
            l_new.append(alpha * l[h] + jnp.sum(p, axis=-1, keepdims=True))
            acc_new.append(alpha * acc[h] + _dot(p.astype(BF16), vt))
            m_new.append(mh)
        return run, tuple(m_new), tuple(l_new), tuple(acc_new)

    col1 = jnp.zeros((Q_BLOCK, 1), F32)
    init = (col1,
            tuple(jnp.full((Q_BLOCK, 1), NEG_BIG, F32) for _ in range(B_HEADS)),
            tuple(col1 for _ in range(B_HEADS)),
            tuple(jnp.zeros((Q_BLOCK, B_HD), F32) for _ in range(B_HEADS)))
    _, _, l, acc = lax.fori_loop(0, n_tiles, attend, init)
    for h in range(B_HEADS):
        o_ref[:, h * B_HD:(h + 1) * B_HD] = (acc[h] / l[h]).astype(o_ref.dtype)


def _dsa(z3, q_gain, k_gain, *, tk):
    bsz, s, _ = z3.shape
    topk = min(TOPK_MAX, s // 4)
    assert tk >= topk and s % tk == 0 and tk % LANES == 0
    n_t = s // tk
    tri = jnp.asarray(np.triu(np.ones((tk, tk), np.float32), 1), BF16)
    kg = jnp.concatenate([k_gain.astype(F32), jnp.ones((LANES - B_HD,), F32)]).reshape(1, LANES)
    w_q = B_HEADS * B_HD
    return pl.pallas_call(
        functools.partial(_dsa_kernel, topk, tk),
        grid=(bsz, s // Q_BLOCK),
        in_specs=[pl.BlockSpec((None, Q_BLOCK, w_q), lambda b, i: (b, i, Z_BQ // w_q)),
                  pl.BlockSpec((None, Q_BLOCK, w_q), lambda b, i: (b, i, Z_IQ // w_q)),
                  pl.BlockSpec((None, s, SM_W), lambda b, i: (b, 0, Z_SM // SM_W)),
                  pl.BlockSpec((1, B_HD), lambda b, i: (0, 0)),
                  pl.BlockSpec((1, LANES), lambda b, i: (0, 0)),
                  pl.BlockSpec((tk, tk), lambda b, i: (0, 0))],
        out_specs=pl.BlockSpec((None, Q_BLOCK, w_q), lambda b, i: (b, i, 0)),
        out_shape=jax.ShapeDtypeStruct((bsz, s, w_q), BF16),
        scratch_shapes=[pltpu.VMEM((n_t, IDX_DIM, tk), BF16),
                        pltpu.VMEM((n_t, B_HD, tk), BF16),
                        pltpu.VMEM((n_t, tk, B_HD), BF16),
                        pltpu.VMEM((n_t, Q_BLOCK, tk), jnp.int32),
                        pltpu.VMEM((B_HEADS * Q_BLOCK, B_HD), BF16),
                        pltpu.VMEM((IDX_HEADS * Q_BLOCK, IDX_DIM), BF16),
                        pltpu.VMEM((IDX_HEADS, Q_BLOCK, LANES), F32)],
        compiler_params=_cparams(("arbitrary", "arbitrary")),
        name="dsa",
    )(z3, z3, z3, q_gain.reshape(1, B_HD).astype(F32), kg, tri)


def _out_proj_kernel(x_ref, a_ref, b_ref, wa_ref, wb_ref, o_ref):
    o_ref[...] = x_ref[...] + _dot(a_ref[...], wa_ref[...]) + _dot(b_ref[...], wb_ref[...])


def _out_proj(x2d, a2d, b2d, w, *, tm, tn):
    m, d = x2d.shape
    ka = a2d.shape[1]
    kb = b2d.shape[1]
    return pl.pallas_call(
        _out_proj_kernel,
        grid=(m // tm, d // tn),
        in_specs=[pl.BlockSpec((tm, tn), lambda i, j: (i, j)),
                  pl.BlockSpec((tm, ka), lambda i, j: (i, 0)),
                  pl.BlockSpec((tm, kb), lambda i, j: (i, 0)),
                  pl.BlockSpec((ka, tn), lambda i, j: (0, j)),
                  pl.BlockSpec((kb, tn), lambda i, j: (0, j))],
        out_specs=pl.BlockSpec((tm, tn), lambda i, j: (i, j)),
        out_shape=jax.ShapeDtypeStruct((m, d), F32),
        compiler_params=_cparams(("arbitrary", "arbitrary")),
        name="even_out_proj",
    )(x2d, a2d, b2d, w[:ka], w[ka:])


def _causal_conv(u, prev, w_ref, b):
    width = w_ref.shape[0]
    rows = u.shape[0]

    def taps(x):
        y = w_ref[width - 1:width, :] * x + b
        for i in range(width - 1):
            y = y + w_ref[i:i + 1, :] * pltpu.roll(x, width - 1 - i, axis=0)
        return y

    head = taps(jnp.concatenate([prev, u[:SUBLANES]], axis=0))[SUBLANES:]
    return jnp.concatenate([head, taps(u)[SUBLANES:]], axis=0) if rows > SUBLANES else head


def _ffn_kernel(x_ref, g_ref, wg_ref, wv_ref, cwg_ref, cwv_ref, cbg_ref, cbv_ref, wd_ref, o_ref,
                xn_ref, acc_ref, pg_ref, pv_ref):
    s = pl.program_id(1)
    j = pl.program_id(2)
    rows = x_ref.shape[0]

    @pl.when(j == 0)
    def _():
        xn_ref[...] = _rms(x_ref[...], g_ref[...]).astype(BF16)
        acc_ref[...] = jnp.zeros_like(acc_ref)

    @pl.when(s == 0)
    def _():
        pg_ref[j] = jnp.zeros(pg_ref.shape[1:], F32)
        pv_ref[j] = jnp.zeros(pv_ref.shape[1:], F32)

    xn = xn_ref[...]

    def branch(w_ref, cw_ref, cb_ref, prev_ref):
        u = _dot(xn, w_ref[...])
        prev = prev_ref[j]
        prev_ref[j] = u[rows - SUBLANES:]
        return _causal_conv(u, prev, cw_ref, cb_ref[...])

    gate = branch(wg_ref, cwg_ref, cbg_ref, pg_ref)
    val = branch(wv_ref, cwv_ref, cbv_ref, pv_ref)
    act = jax.nn.gelu(gate, approximate=True) * val
    acc_ref[...] += _dot(act.astype(BF16), wd_ref[...])

    @pl.when(j == pl.num_programs(2) - 1)
    def _():
        o_ref[...] = x_ref[...] + acc_ref[...]


def _conv_ffn(h3, gain, w_up, conv_w, conv_b, w_down, *, ts, tf):
    bsz, s, d = h3.shape
    dff = w_down.shape[0]
    nj = dff // tf
    width = conv_w.shape[0]
    cb = conv_b.reshape(1, -1)
    xspec = pl.BlockSpec((None, ts, d), lambda b, i, j: (b, i, 0))
    return pl.pallas_call(
        _ffn_kernel,
        grid=(bsz, s // ts, nj),
        in_specs=[xspec,
                  pl.BlockSpec((1, d), lambda b, i, j: (0, 0)),
                  pl.BlockSpec((d, tf), lambda b, i, j: (0, j)),
                  pl.BlockSpec((d, tf), lambda b, i, j: (0, j + nj)),
                  pl.BlockSpec((width, tf), lambda b, i, j: (0, j)),
                  pl.BlockSpec((width, tf), lambda b, i, j: (0, j + nj)),
                  pl.BlockSpec((1, tf), lambda b, i, j: (0, j)),
                  pl.BlockSpec((1, tf), lambda b, i, j: (0, j + nj)),
                  pl.BlockSpec((tf, d), lambda b, i, j: (j, 0))],
        out_specs=xspec,
        out_shape=jax.ShapeDtypeStruct((bsz, s, d), F32),
        scratch_shapes=[pltpu.VMEM((ts, d), BF16),
                        pltpu.VMEM((ts, d), F32),
                        pltpu.VMEM((nj, SUBLANES, tf), F32),
                        pltpu.VMEM((nj, SUBLANES, tf), F32)],
        compiler_params=_cparams(("arbitrary", "arbitrary", "arbitrary")),
        name="conv_ffn",
    )(h3, gain.reshape(1, d), w_up, w_up, conv_w, conv_w, cb, cb, w_down)


def _rglru_kernel(x_ref, g_ref, win_ref, cw_ref, cb_ref, wa_ref, ba_ref, wx_ref, bx_ref, lam_ref,
                  wout_ref, o_ref, a_ref, u_ref, y_ref, px_ref, ph_ref):
    rows = x_ref.shape[0]
    width = LRU_BLOCKS * LRU_BW

    @pl.when(pl.program_id(1) == 0)
    def _():
        px_ref[...] = jnp.zeros_like(px_ref)
        ph_ref[...] = jnp.zeros_like(ph_ref)

    xn = _rms(x_ref[...], g_ref[...]).astype(BF16)
    lam = lam_ref[...]
    log_sig = jnp.minimum(lam, 0.0) - jnp.log1p(jnp.exp(-jnp.abs(lam)))
    for n in range(LRU_BLOCKS):
        cs = slice(n * LRU_BW, (n + 1) * LRU_BW)
        yb = _dot(xn, win_ref[:, cs])
        xb = _dot(xn, win_ref[:, width + n * LRU_BW:width + (n + 1) * LRU_BW])
        prev = px_ref[:, cs]
        px_ref[:, cs] = xb[rows - SUBLANES:]
        xc = _causal_conv(xb, prev, cw_ref.at[:, cs], cb_ref[:, cs])
        xcb = xc.astype(BF16)
        r = _sigmoid(_dot(xcb, wa_ref[n]) + ba_ref[:, cs])
        gi = _sigmoid(_dot(xcb, wx_ref[n]) + bx_ref[:, cs])
        log_a = LRU_C * r * log_sig[:, cs]
        a = jnp.exp(log_a)
        a_ref[:, cs] = a
        u_ref[:, cs] = jnp.sqrt(-jnp.tanh(log_a) * (a * a + 1.0)) * (gi * xc)
        y_ref[:, cs] = jax.nn.gelu(yb, approximate=True)

    def step(t, h):
        h = a_ref[pl.ds(t, 1), :] * h + u_ref[pl.ds(t, 1), :]
        u_ref[pl.ds(t, 1), :] = h
        return h

    h = lax.fori_loop(0, rows, step, ph_ref[0:1, :], unroll=8)
    ph_ref[0:1, :] = h
    o_ref[...] = x_ref[...] + _dot((u_ref[...] * y_ref[...]).astype(BF16), wout_ref[...])


def _rglru(h3, gain, w_in, conv_w, conv_b, wa, ba, wx, bx, lam, w_out, *, ts):
    bsz, s, d = h3.shape
    width = LRU_BLOCKS * LRU_BW
    xspec = pl.BlockSpec((None, ts, d), lambda b, i: (b, i, 0))

    def full(a):
        return pl.BlockSpec(a.shape, lambda b, i: (0,) * a.ndim)

    args = (gain.reshape(1, d), w_in, conv_w, conv_b.reshape(1, width), wa, ba.reshape(1, width),
            wx, bx.reshape(1, width), lam.reshape(1, width), w_out)
    return pl.pallas_call(
        _rglru_kernel,
        grid=(bsz, s // ts),
        in_specs=[xspec] + [full(a) for a in args],
        out_specs=xspec,
        out_shape=jax.ShapeDtypeStruct((bsz, s, d), F32),
        scratch_shapes=[pltpu.VMEM((ts, width), F32),
                        pltpu.VMEM((ts, width), F32),
                        pltpu.VMEM((ts, width), F32),
                        pltpu.VMEM((SUBLANES, width), F32),
                        pltpu.VMEM((SUBLANES, width), F32)],
        compiler_params=_cparams(("arbitrary", "arbitrary")),
        name="rglru_block",
    )(h3, *args)


def _tile(n, pref):
    t = min(n, pref)
    assert n % t == 0
    return t


def kernel(x, lb_logits, even_norm, even_w_in, even_w_out, a_out_norm, b_q_norm, b_k_norm, odd_norm, odd_w_in, odd_conv_w, odd_conv_b, odd_gate_a_w, odd_gate_a_b, odd_gate_x_w, odd_gate_x_b, odd_lambda, odd_w_out, ffn_norm, ffn_w_up, ffn_conv_w, ffn_conv_b, ffn_w_down):
    bsz, s, d = x.shape
    depth = ffn_norm.shape[0]
    h = x
    for layer in range(depth):
        j = layer // 2
        if layer % 2 == 0:
            w = even_w_in[j]
            o_bk = 2560
            o_iq = o_bk + 2 * B_HD
            o_ik = o_iq + IDX_HEADS * IDX_DIM
            o_end = o_ik + IDX_DIM + IDX_HEADS
            w = jnp.concatenate(
                [w[:, :o_bk], w[:, o_iq:o_ik], w[:, o_bk:o_iq], w[:, o_ik:o_end],
                 jnp.zeros((d, Z_W - o_end), w.dtype)], axis=1).astype(BF16)
            z = _norm_proj(h.reshape(bsz * s, d), even_norm[j], w,
                           tm=_tile(bsz * s, 512), tn=_tile(Z_W, 1664), out_dtype=F32)
            z3 = z.reshape(bsz, s, Z_W)
            a_o = _hgrn(z3, lb_logits, a_out_norm[j], layer, t_rows=_tile(s, 512))
            b_o = _dsa(z3, b_q_norm[j], b_k_norm[j], tk=_tile(s, 512))
            h = _out_proj(h.reshape(bsz * s, d), a_o.reshape(bsz * s, -1), b_o.reshape(bsz * s, -1),
                          even_w_out[j].astype(BF16), tm=_tile(bsz * s, 512), tn=d).reshape(bsz, s, d)
        else:
            h = _rglru(h, odd_norm[j], odd_w_in[j].astype(BF16), odd_conv_w[j], odd_conv_b[j],
                       odd_gate_a_w[j].astype(BF16), odd_gate_a_b[j], odd_gate_x_w[j].astype(BF16),
                       odd_gate_x_b[j], odd_lambda[j], odd_w_out[j].astype(BF16), ts=_tile(s, 512))
        h = _conv_ffn(h, ffn_norm[layer], ffn_w_up[layer].astype(BF16), ffn_conv_w[layer],
                      ffn_conv_b[layer], ffn_w_down[layer].astype(BF16), ts=_tile(s, 1024), tf=512)
    return h
```

```python
import functools

import numpy as np
import jax
import jax.numpy as jnp
from jax import lax
from jax.experimental import pallas as pl
from jax.experimental.pallas import tpu as pltpu

F32 = jnp.float32
BF16 = jnp.bfloat16

EPS = 1e-6
CHUNK = 64
A_HEADS = 4
A_D = 128
B_HEADS = 8
B_HD = 64
IDX_HEADS = 8
IDX_DIM = 64
TOPK_MAX = 256
LRU_BLOCKS = 10
LRU_BW = 128
LRU_C = 8.0
LANES = 128
SUBLANES = 8
VMEM_LIMIT = 56 * 1024 * 1024

Z_A = 0
Z_BQ = 2048
Z_IQ = 2560
Z_SM = 3072
Z_W = 3328
SM_W = 256


def _cparams(sem):
    return pltpu.CompilerParams(dimension_semantics=sem, vmem_limit_bytes=VMEM_LIMIT)


def _rms(x, gain):
    ms = jnp.mean(x * x, axis=-1, keepdims=True)
    return x * lax.rsqrt(ms + EPS) * gain


def _sigmoid(x):
    return 1.0 / (1.0 + jnp.exp(-x))


def _dot(a, b):
    return jnp.dot(a, b, preferred_element_type=F32)


def _dot_nt(a, b):
    return lax.dot_general(a, b, (((1,), (1,)), ((), ())), preferred_element_type=F32)


def _dot_tn(a, b):
    return lax.dot_general(a, b, (((0,), (0,)), ((), ())), preferred_element_type=F32)


def _norm_proj_kernel(x_ref, g_ref, w_ref, o_ref, xn_ref):
    @pl.when(pl.program_id(1) == 0)
    def _():
        xn_ref[...] = _rms(x_ref[...], g_ref[...]).astype(BF16)

    o_ref[...] = _dot(xn_ref[...], w_ref[...]).astype(o_ref.dtype)


def _norm_proj(x2d, gain, w, *, tm, tn, out_dtype):
    m, d = x2d.shape
    n = w.shape[1]
    return pl.pallas_call(
        _norm_proj_kernel,
        grid=(m // tm, n // tn),
        in_specs=[pl.BlockSpec((tm, d), lambda i, j: (i, 0)),
                  pl.BlockSpec((1, d), lambda i, j: (0, 0)),
                  pl.BlockSpec((d, tn), lambda i, j: (0, j))],
        out_specs=pl.BlockSpec((tm, tn), lambda i, j: (i, j)),
        out_shape=jax.ShapeDtypeStruct((m, n), out_dtype),
        scratch_shapes=[pltpu.VMEM((tm, d), BF16)],
        compiler_params=_cparams(("arbitrary", "arbitrary")),
        name="even_in_proj",
    )(x2d, gain.reshape(1, d), w)


def _hgrn_consts(c):
    levels = int(np.log2(c))
    tri = np.tril(np.ones((c, c), np.float32))
    mats = [tri]
    masks = []
    t = np.arange(c)
    for l in range(levels):
        h = c >> (l + 1)
        base = (t // (2 * h)) * (2 * h)
        mid = base + h - 1
        mats.append(tri[mid])
        second = (t % (2 * h)) >= h
        same = base[:, None] == base[None, :]
        masks.append((same & second[:, None] & (~second)[None, :]).astype(np.float32))
    masks.append(np.eye(c, dtype=np.float32))
    return np.concatenate(mats, 0), np.stack(masks, 0), levels


def _hgrn_kernel(layer, c, levels, q_ref, f_ref, i_ref, g_ref, lbl_ref, an_ref, sel_ref, msk_ref,
                 o_ref, st_ref):
    @pl.when(pl.program_id(2) == 0)
    def _():
        st_ref[...] = jnp.zeros_like(st_ref)

    lg = lbl_ref[...]
    e = jnp.exp(lg - jnp.max(lg, axis=0, keepdims=True))
    p = e / jnp.sum(e, axis=0, keepdims=True)
    lb = jnp.sum(p[:layer + 1], axis=0, keepdims=True)
    an = an_ref[...]
    n_chunks = q_ref.shape[0] // c

    def chunk(ci, carry):
        r0 = pl.multiple_of(ci * c, c)
        qr = q_ref[pl.ds(r0, c), :].astype(F32)
        fr = f_ref[pl.ds(r0, c), :].astype(F32)
        v = i_ref[pl.ds(r0, c), :].astype(BF16)
        gr = g_ref[pl.ds(r0, c), :].astype(F32)
        q = qr * _sigmoid(qr)
        f = lb + (1.0 - lb) * _sigmoid(fr)
        g = jnp.log(f)
        k = 1.0 - f
        g1 = g.astype(BF16)
        r1 = g - g1.astype(F32)
        g2 = r1.astype(BF16)
        g3 = (r1 - g2.astype(F32)).astype(BF16)
        rs = _dot(sel_ref[...], jnp.concatenate([g1, g2, g3], axis=1))
        rs = rs[:, :A_D] + rs[:, A_D:2 * A_D] + rs[:, 2 * A_D:]
        b = rs[:c]
        scores = msk_ref[levels] * _dot_nt(q.astype(BF16), k.astype(BF16))
        for l in range(levels):
            x = jnp.exp(-jnp.abs(b - rs[(l + 1) * c:(l + 2) * c]))
            scores = scores + msk_ref[l] * _dot_nt((q * x).astype(BF16), (k * x).astype(BF16))
        st = st_ref[...]
        inter = _dot_nt((q * jnp.exp(b)).astype(BF16), st.astype(BF16))
        o = inter + _dot(scores.astype(BF16), v)
        b_last = b[c - 1:c, :]
        kd = (k * jnp.exp(b_last - b)).astype(BF16)
        st_ref[...] = jnp.exp(b_last) * st + _dot_tn(v, kd)
        o = _rms(o, an) * (gr * _sigmoid(gr))
        o_ref[pl.ds(r0, c), :] = o.astype(o_ref.dtype)
        return carry

    lax.fori_loop(0, n_chunks, chunk, 0)


def _hgrn(z3, lb_logits, a_norm, layer, *, t_rows):
    bsz, s, _ = z3.shape
    sel, msk, levels = _hgrn_consts(CHUNK)
    sel = jnp.asarray(sel, BF16)
    msk = jnp.asarray(msk, F32)
    nl = lb_logits.shape[0]
    col = Z_A // A_D

    def zspec(k):
        return pl.BlockSpec((None, t_rows, A_D), lambda b, h, t: (b, t, col + k * A_HEADS + h))

    return pl.pallas_call(
        functools.partial(_hgrn_kernel, layer, CHUNK, levels),
        grid=(bsz, A_HEADS, s // t_rows),
        in_specs=[zspec(0), zspec(1), zspec(2), zspec(3),
                  pl.BlockSpec((nl, A_D), lambda b, h, t: (0, h)),
                  pl.BlockSpec((1, A_D), lambda b, h, t: (0, h)),
                  pl.BlockSpec(sel.shape, lambda b, h, t: (0, 0)),
                  pl.BlockSpec(msk.shape, lambda b, h, t: (0, 0, 0))],
        out_specs=pl.BlockSpec((None, t_rows, A_D), lambda b, h, t: (b, t, h)),
        out_shape=jax.ShapeDtypeStruct((bsz, s, A_HEADS * A_D), BF16),
        scratch_shapes=[pltpu.VMEM((A_D, A_D), F32)],
        compiler_params=_cparams(("arbitrary", "arbitrary", "arbitrary")),
        name="hgrn2",
    )(z3, z3, z3, z3, lb_logits, a_norm.reshape(1, -1), sel, msk)


INT_MIN = -2 ** 31
KEY_MASK = 0x7FFFFFFF
N_PAIR = B_HEADS // 2
MASKED = -1e30
M_INIT = -1e28
COUNT_ROWS = 128


def _sort_key(x):
    bits = pltpu.bitcast(x, jnp.int32)
    key = bits ^ ((bits >> 31) & KEY_MASK)
    return jnp.where(x == 0.0, 0, key)


def _dsa_kernel(topk, tk, zq_ref, zi_ref, zs_ref, qg_ref, kg_ref, bd_ref, tri_ref, ones_ref, o_ref,
                kibd_ref, knbd_ref, vbd_ref, key_ref, qa_ref, qi_ref, wb_ref, p_ref):
    qb = pl.program_id(1)
    qrows = zq_ref.shape[0]
    n_tiles_all = kibd_ref.shape[0]
    lane = lax.broadcasted_iota(jnp.int32, (1, LANES), 1)
    lo_half = lane < B_HD
    reps = tk // LANES

    @pl.when(qb == 0)
    def _():
        kg = kg_ref[...]
        top = lax.broadcasted_iota(jnp.int32, (LANES, 1), 0) < B_HD

        def both_halves(x):
            return jnp.where(lo_half, x, pltpu.roll(x, B_HD, axis=1))

        def prep(t, carry):
            r0 = pl.multiple_of(t * tk, tk)
            rows = zs_ref[pl.ds(r0, tk), :].astype(F32)
            kv = rows[:, :LANES]
            ms = jnp.sum(jnp.where(lo_half, kv * kv, 0.0), axis=-1, keepdims=True) * (1.0 / B_HD)
            knt = both_halves(kv * lax.rsqrt(ms + EPS) * kg).T
            knbd_ref[t, :, :tk] = jnp.where(top, knt, 0.0).astype(BF16)
            knbd_ref[t, :, tk:] = jnp.where(top, 0.0, knt).astype(BF16)
            kit = both_halves(rows[:, LANES:]).T
            kibd_ref[t, :, :tk] = jnp.where(top, kit, 0.0).astype(BF16)
            kibd_ref[t, :, tk:] = jnp.where(top, 0.0, kit).astype(BF16)
            va = jnp.where(lo_half, pltpu.roll(kv, B_HD, axis=1), jnp.where(lane == B_HD, 1.0, 0.0))
            vb = jnp.where(lo_half, jnp.where(lane == 0, 1.0, 0.0), kv)
            zero = jnp.zeros((tk, LANES), BF16)
            vbd_ref[t, :tk, :LANES] = va.astype(BF16)
            vbd_ref[t, :tk, LANES:] = zero
            vbd_ref[t, tk:, :LANES] = zero
            vbd_ref[t, tk:, LANES:] = vb.astype(BF16)
            return carry

        lax.fori_loop(0, n_tiles_all, prep, 0)

    q = zq_ref[...].astype(F32)
    sq = q * q
    sq_hi = sq.astype(BF16)
    sq_lo = (sq - sq_hi.astype(F32)).astype(BF16)
    ssum = _dot(sq_hi, bd_ref[...]) + _dot(sq_lo, bd_ref[...])
    qn = (q * lax.rsqrt(ssum * (1.0 / B_HD) + EPS) * qg_ref[...]).astype(BF16)
    qi = zi_ref[...].astype(BF16)
    for j in range(N_PAIR):
        qa_ref[j * qrows:(j + 1) * qrows, :] = qn[:, j * LANES:(j + 1) * LANES]
        qi_ref[j * qrows:(j + 1) * qrows, :] = qi[:, j * LANES:(j + 1) * LANES]
    q0 = pl.multiple_of(qb * qrows, qrows)
    w = zs_ref[pl.ds(q0, qrows), :].astype(F32)[:, 3 * B_HD:3 * B_HD + IDX_HEADS]
    w = w * ((IDX_HEADS ** -0.5) * (IDX_DIM ** -0.5))
    for h in range(IDX_HEADS):
        wb_ref[h] = jnp.broadcast_to(w[:, h:h + 1], (qrows, LANES))

    n_tiles = (q0 + qrows + tk - 1) // tk
    row = lax.broadcasted_iota(jnp.int32, (qrows, 1), 0)
    limit = q0 + ((row // CHUNK) + 1) * CHUNK

    def admissible(t):
        col = t * tk + lax.broadcasted_iota(jnp.int32, (qrows, tk), 1)
        return col < limit

    def head_block(x, h):
        j, half = divmod(h, 2)
        return x[j * qrows:(j + 1) * qrows, half * tk:(half + 1) * tk]

    def index_tile(t, carry):
        d = _dot(qi_ref[...], kibd_ref[t])
        acc = jnp.zeros((qrows, tk), F32)
        for h in range(IDX_HEADS):
            wh = pltpu.repeat(wb_ref[h], reps, axis=1)
            acc = acc + jnp.maximum(head_block(d, h), 0.0) * wh
        key_ref[t] = jnp.where(admissible(t), _sort_key(acc), INT_MIN)
        return carry

    lax.fori_loop(0, n_tiles, index_tile, 0)

    def count_ge(cand):
        accs = []
        for r0 in range(0, qrows, COUNT_ROWS):
            cr = cand[r0:r0 + COUNT_ROWS]

            def body(t, acc, r0=r0, cr=cr):
                for c in range(reps):
                    keys = key_ref[t, r0:r0 + COUNT_ROWS, c * LANES:(c + 1) * LANES]
                    acc = acc + jnp.where(keys >= cr, 1.0, 0.0)
                return acc
            accs.append(lax.fori_loop(0, n_tiles, body, jnp.zeros((COUNT_ROWS, LANES), F32)))
        return _dot(jnp.concatenate(accs, axis=0).astype(BF16), ones_ref[...])

    def bit_step(i, thr):
        cand = thr + lax.shift_left(jnp.int32(1), 31 - i)
        return jnp.where(count_ge(cand) >= float(topk), cand, thr)

    thr = lax.fori_loop(0, 32, bit_step, jnp.full((qrows, LANES), INT_MIN, jnp.int32))
    need = float(topk) - count_ge(thr + 1)
    need = jnp.where(thr == INT_MIN, 0.0, need)
    need_t = pltpu.repeat(need, reps, axis=1)
    thr_t = pltpu.repeat(thr, reps, axis=1)

    def attend(t, carry):
        run, m, acc = carry
        keys = key_ref[t]
        eq = keys == thr_t
        eqf = jnp.where(eq, 1.0, 0.0)
        rank = _dot(eqf.astype(BF16), tri_ref[...]) + run
        run = run + jnp.sum(eqf, axis=-1, keepdims=True)
        tie_ok = jnp.where(eq, rank, float(topk)) < need_t
        bias = jnp.where(keys > thr_t, 0.0, jnp.where(tie_ok, 0.0, MASKED))
        s_all = _dot(qa_ref[...], knbd_ref[t])
        m_new, alpha = [], []
        for h in range(B_HEADS):
            j, half = divmod(h, 2)
            s = head_block(s_all, h) + bias
            mh = jnp.maximum(m[h], jnp.max(s, axis=-1, keepdims=True))
            p_ref[j * qrows:(j + 1) * qrows, half * tk:(half + 1) * tk] = jnp.exp(s - mh).astype(BF16)
            alpha.append(jnp.exp(m[h] - mh))
            m_new.append(mh)
        pv = _dot(p_ref[...], vbd_ref[t])
        acc_new = []
        for j in range(N_PAIR):
            al = jnp.concatenate([jnp.broadcast_to(alpha[2 * j], (qrows, LANES)),
                                  jnp.broadcast_to(alpha[2 * j + 1], (qrows, LANES))], axis=1)
            acc_new.append(al * acc[j] + pv[j * qrows:(j + 1) * qrows])
        return run, tuple(m_new), tuple(acc_new)

    init = (jnp.zeros((qrows, 1), F32),
            tuple(jnp.full((qrows, 1), M_INIT, F32) for _ in range(B_HEADS)),
            tuple(jnp.zeros((qrows, 2 * LANES), F32) for _ in range(N_PAIR)))
    _, _, acc = lax.fori_loop(0, n_tiles, attend, init)
    for j in range(N_PAIR):
        a = acc[j]
        o = jnp.where(lo_half, a[:, :LANES] / a[:, B_HD:B_HD + 1], a[:, LANES:] / a[:, LANES:LANES + 1])
        o_ref[:, j * LANES:(j + 1) * LANES] = o.astype(o_ref.dtype)


def _dsa(z3, q_gain, k_gain, *, tk, qrows):
    bsz, s, _ = z3.shape
    topk = min(TOPK_MAX, s // 4)
    assert tk >= topk >= 2 and s % tk == 0 and tk % LANES == 0 and s % qrows == 0 and qrows % CHUNK == 0
    n_t = s // tk
    w_q = B_HEADS * B_HD
    tri = jnp.asarray(np.triu(np.ones((tk, tk), np.float32), 1), BF16)
    head_of = np.arange(w_q) // B_HD
    bd = jnp.asarray(head_of[:, None] == head_of[None, :], BF16)
    ones = jnp.ones((LANES, LANES), BF16)
    assert s // LANES <= 256 and qrows % COUNT_ROWS == 0
    kg = jnp.concatenate([k_gain.astype(F32), jnp.ones((LANES - B_HD,), F32)]).reshape(1, LANES)
    qg = jnp.tile(q_gain.astype(F32) * (B_HD ** -0.5), B_HEADS).reshape(1, w_q)

    def const(a):
        return pl.BlockSpec(a.shape, lambda b, i: (0,) * a.ndim)

    return pl.pallas_call(
        functools.partial(_dsa_kernel, topk, tk),
        grid=(bsz, s // qrows),
        in_specs=[pl.BlockSpec((None, qrows, w_q), lambda b, i: (b, i, Z_BQ // w_q)),
                  pl.BlockSpec((None, qrows, w_q), lambda b, i: (b, i, Z_IQ // w_q)),
                  pl.BlockSpec((None, s, SM_W), lambda b, i: (b, 0, Z_SM // SM_W)),
                  const(qg), const(kg), const(bd), const(tri), const(ones)],
        out_specs=pl.BlockSpec((None, qrows, w_q), lambda b, i: (b, i, 0)),
        out_shape=jax.ShapeDtypeStruct((bsz, s, w_q), BF16),
        scratch_shapes=[pltpu.VMEM((n_t, LANES, 2 * tk), BF16),
                        pltpu.VMEM((n_t, LANES, 2 * tk), BF16),
                        pltpu.VMEM((n_t, 2 * tk, 2 * LANES), BF16),
                        pltpu.VMEM((n_t, qrows, tk), jnp.int32),
                        pltpu.VMEM((N_PAIR * qrows, LANES), BF16),
                        pltpu.VMEM((N_PAIR * qrows, LANES), BF16),
                        pltpu.VMEM((IDX_HEADS, qrows, LANES), F32),
                        pltpu.VMEM((N_PAIR * qrows, 2 * tk), BF16)],
        compiler_params=_cparams(("arbitrary", "arbitrary")),
        name="dsa",
    )(z3, z3, z3, qg, kg, bd, tri, ones)


def _out_proj_kernel(x_ref, a_ref, b_ref, wa_ref, wb_ref, o_ref):
    o_ref[...] = x_ref[...] + _dot(a_ref[...], wa_ref[...]) + _dot(b_ref[...], wb_ref[...])


def _out_proj(x2d, a2d, b2d, w, *, tm, tn):
    m, d = x2d.shape
    ka = a2d.shape[1]
    kb = b2d.shape[1]
    return pl.pallas_call(
        _out_proj_kernel,
        grid=(m // tm, d // tn),
        in_specs=[pl.BlockSpec((tm, tn), lambda i, j: (i, j)),
                  pl.BlockSpec((tm, ka), lambda i, j: (i, 0)),
                  pl.BlockSpec((tm, kb), lambda i, j: (i, 0)),
                  pl.BlockSpec((ka, tn), lambda i, j: (0, j)),
                  pl.BlockSpec((kb, tn), lambda i, j: (0, j))],
        out_specs=pl.BlockSpec((tm, tn), lambda i, j: (i, j)),
        out_shape=jax.ShapeDtypeStruct((m, d), F32),
        compiler_params=_cparams(("arbitrary", "arbitrary")),
        name="even_out_proj",
    )(x2d, a2d, b2d, w[:ka], w[ka:])


def _causal_conv(u, prev, w_ref, b):
    width = w_ref.shape[0]
    rows = u.shape[0]

    def taps(x):
        y = w_ref[width - 1:width, :] * x + b
        for i in range(width - 1):
            y = y + w_ref[i:i + 1, :] * pltpu.roll(x, width - 1 - i, axis=0)
        return y

    head = taps(jnp.concatenate([prev, u[:SUBLANES]], axis=0))[SUBLANES:]
    return jnp.concatenate([head, taps(u)[SUBLANES:]], axis=0) if rows > SUBLANES else head


def _ffn_kernel(x_ref, g_ref, wg_ref, wv_ref, cwg_ref, cwv_ref, cbg_ref, cbv_ref, wd_ref, o_ref,
                xn_ref, acc_ref, pg_ref, pv_ref):
    s = pl.program_id(1)
    j = pl.program_id(2)
    rows = x_ref.shape[0]

    @pl.when(j == 0)
    def _():
        xn_ref[...] = _rms(x_ref[...], g_ref[...]).astype(BF16)
        acc_ref[...] = jnp.zeros_like(acc_ref)

    @pl.when(s == 0)
    def _():
        pg_ref[j] = jnp.zeros(pg_ref.shape[1:], F32)
        pv_ref[j] = jnp.zeros(pv_ref.shape[1:], F32)

    xn = xn_ref[...]

    def branch(w_ref, cw_ref, cb_ref, prev_ref):
        u = _dot(xn, w_ref[...])
        prev = prev_ref[j]
        prev_ref[j] = u[rows - SUBLANES:]
        return _causal_conv(u, prev, cw_ref, cb_ref[...])

    gate = branch(wg_ref, cwg_ref, cbg_ref, pg_ref)
    val = branch(wv_ref, cwv_ref, cbv_ref, pv_ref)
    act = jax.nn.gelu(gate, approximate=True) * val
    acc_ref[...] += _dot(act.astype(BF16), wd_ref[...])

    @pl.when(j == pl.num_programs(2) - 1)
    def _():
        o_ref[...] = x_ref[...] + acc_ref[...]


def _conv_ffn(h3, gain, w_up, conv_w, conv_b, w_down, *, ts, tf):
    bsz, s, d = h3.shape
    dff = w_down.shape[0]
    nj = dff // tf
    width = conv_w.shape[0]
    cb = conv_b.reshape(1, -1)
    xspec = pl.BlockSpec((None, ts, d), lambda b, i, j: (b, i, 0))
    return pl.pallas_call(
        _ffn_kernel,
        grid=(bsz, s // ts, nj),
        in_specs=[xspec,
                  pl.BlockSpec((1, d), lambda b, i, j: (0, 0)),
                  pl.BlockSpec((d, tf), lambda b, i, j: (0, j)),
                  pl.BlockSpec((d, tf), lambda b, i, j: (0, j + nj)),
                  pl.BlockSpec((width, tf), lambda b, i, j: (0, j)),
                  pl.BlockSpec((width, tf), lambda b, i, j: (0, j + nj)),
                  pl.BlockSpec((1, tf), lambda b, i, j: (0, j)),
                  pl.BlockSpec((1, tf), lambda b, i, j: (0, j + nj)),
                  pl.BlockSpec((tf, d), lambda b, i, j: (j, 0))],
        out_specs=xspec,
        out_shape=jax.ShapeDtypeStruct((bsz, s, d), F32),
        scratch_shapes=[pltpu.VMEM((ts, d), BF16),
                        pltpu.VMEM((ts, d), F32),
                        pltpu.VMEM((nj, SUBLANES, tf), F32),
                        pltpu.VMEM((nj, SUBLANES, tf), F32)],
        compiler_params=_cparams(("arbitrary", "arbitrary", "arbitrary")),
        name="conv_ffn",
    )(h3, gain.reshape(1, d), w_up, w_up, conv_w, conv_w, cb, cb, w_down)


def _rglru_kernel(x_ref, g_ref, win_ref, cw_ref, cb_ref, wa_ref, ba_ref, wx_ref, bx_ref, lam_ref,
                  wout_ref, o_ref, a_ref, u_ref, y_ref, px_ref, ph_ref):
    rows = x_ref.shape[0]
    width = LRU_BLOCKS * LRU_BW

    @pl.when(pl.program_id(1) == 0)
    def _():
        px_ref[...] = jnp.zeros_like(px_ref)
        ph_ref[...] = jnp.zeros_like(ph_ref)

    xn = _rms(x_ref[...], g_ref[...]).astype(BF16)
    lam = lam_ref[...]
    log_sig = jnp.minimum(lam, 0.0) - jnp.log1p(jnp.exp(-jnp.abs(lam)))
    for n in range(LRU_BLOCKS):
        cs = slice(n * LRU_BW, (n + 1) * LRU_BW)
        yb = _dot(xn, win_ref[:, cs])
        xb = _dot(xn, win_ref[:, width + n * LRU_BW:width + (n + 1) * LRU_BW])
        prev = px_ref[:, cs]
        px_ref[:, cs] = xb[rows - SUBLANES:]
        xc = _causal_conv(xb, prev, cw_ref.at[:, cs], cb_ref[:, cs])
        xcb = xc.astype(BF16)
        r = _sigmoid(_dot(xcb, wa_ref[n]) + ba_ref[:, cs])
        gi = _sigmoid(_dot(xcb, wx_ref[n]) + bx_ref[:, cs])
        log_a = LRU_C * r * log_sig[:, cs]
        a = jnp.exp(log_a)
        a_ref[:, cs] = a
        u_ref[:, cs] = jnp.sqrt(-jnp.tanh(log_a) * (a * a + 1.0)) * (gi * xc)
        y_ref[:, cs] = jax.nn.gelu(yb, approximate=True)

    def step(t, h):
        h = a_ref[pl.ds(t, 1), :] * h + u_ref[pl.ds(t, 1), :]
        u_ref[pl.ds(t, 1), :] = h
        return h

    h = lax.fori_loop(0, rows, step, ph_ref[0:1, :], unroll=8)
    ph_ref[0:1, :] = h
    o_ref[...] = x_ref[...] + _dot((u_ref[...] * y_ref[...]).astype(BF16), wout_ref[...])


def _rglru(h3, gain, w_in, conv_w, conv_b, wa, ba, wx, bx, lam, w_out, *, ts):
    bsz, s, d = h3.shape
    width = LRU_BLOCKS * LRU_BW
    xspec = pl.BlockSpec((None, ts, d), lambda b, i: (b, i, 0))

    def full(a):
        return pl.BlockSpec(a.shape, lambda b, i: (0,) * a.ndim)

    args = (gain.reshape(1, d), w_in, conv_w, conv_b.reshape(1, width), wa, ba.reshape(1, width),
            wx, bx.reshape(1, width), lam.reshape(1, width), w_out)
    return pl.pallas_call(
        _rglru_kernel,
        grid=(bsz, s // ts),
        in_specs=[xspec] + [full(a) for a in args],
        out_specs=xspec,
        out_shape=jax.ShapeDtypeStruct((bsz, s, d), F32),
        scratch_shapes=[pltpu.VMEM((ts, width), F32),
                        pltpu.VMEM((ts, width), F32),
                        pltpu.VMEM((ts, width), F32),
                        pltpu.VMEM((SUBLANES, width), F32),
                        pltpu.VMEM((SUBLANES, width), F32)],
        compiler_params=_cparams(("arbitrary", "arbitrary")),
        name="rglru_block",
    )(h3, *args)


def _tile(n, pref):
    t = min(n, pref)
    assert n % t == 0
    return t


def kernel(x, lb_logits, even_norm, even_w_in, even_w_out, a_out_norm, b_q_norm, b_k_norm, odd_norm, odd_w_in, odd_conv_w, odd_conv_b, odd_gate_a_w, odd_gate_a_b, odd_gate_x_w, odd_gate_x_b, odd_lambda, odd_w_out, ffn_norm, ffn_w_up, ffn_conv_w, ffn_conv_b, ffn_w_down):
    bsz, s, d = x.shape
    depth = ffn_norm.shape[0]
    h = x
    for layer in range(depth):
        j = layer // 2
        if layer % 2 == 0:
            w = even_w_in[j]
            o_bk = 2560
            o_iq = o_bk + 2 * B_HD
            o_ik = o_iq + IDX_HEADS * IDX_DIM
            o_end = o_ik + IDX_DIM + IDX_HEADS
            w = jnp.concatenate(
                [w[:, :o_bk], w[:, o_iq:o_ik], w[:, o_bk:o_iq], w[:, o_ik:o_end],
                 jnp.zeros((d, Z_W - o_end), w.dtype)], axis=1).astype(BF16)
            z = _norm_proj(h.reshape(bsz * s, d), even_norm[j], w,
                           tm=_tile(bsz * s, 512), tn=_tile(Z_W, 1664), out_dtype=F32)
            z3 = z.reshape(bsz, s, Z_W)
            a_o = _hgrn(z3, lb_logits, a_out_norm[j], layer, t_rows=_tile(s, 512))
            b_o = _dsa(z3, b_q_norm[j], b_k_norm[j], tk=_tile(s, 512), qrows=_tile(s, 256))
            h = _out_proj(h.reshape(bsz * s, d), a_o.reshape(bsz * s, -1), b_o.reshape(bsz * s, -1),
                          even_w_out[j].astype(BF16), tm=_tile(bsz * s, 512), tn=d).reshape(bsz, s, d)
        else:
            h = _rglru(h, odd_norm[j], odd_w_in[j].astype(BF16), odd_conv_w[j], odd_conv_b[j],
                       odd_gate_a_w[j].astype(BF16), odd_gate_a_b[j], odd_gate_x_w[j].astype(BF16),
                       odd_gate_x_b[j], odd_lambda[j], odd_w_out[j].astype(BF16), ts=_tile(s, 512))
        h = _conv_ffn(h, ffn_norm[layer], ffn_w_up[layer].astype(BF16), ffn_conv_w[layer],
                      ffn_conv_b[layer], ffn_w_down[layer].astype(BF16), ts=_tile(s, 1024), tf=512)
    return h
```

```python
import functools

import numpy as np
import jax
import jax.numpy as jnp
from jax import lax
from jax.experimental import pallas as pl
from jax.experimental.pallas import tpu as pltpu

F32 = jnp.float32
BF16 = jnp.bfloat16

EPS = 1e-6
CHUNK = 64
A_HEADS = 4
A_D = 128
B_HEADS = 8
B_HD = 64
IDX_HEADS = 8
IDX_DIM = 64
TOPK_MAX = 256
LRU_BLOCKS = 10
LRU_BW = 128
LRU_C = 8.0
LANES = 128
SUBLANES = 8
VMEM_LIMIT = 56 * 1024 * 1024

Z_A = 0
Z_BQ = 2048
Z_IQ = 2560
Z_SM = 3072
Z_W = 3328
SM_W = 256


def _cparams(sem):
    return pltpu.CompilerParams(dimension_semantics=sem, vmem_limit_bytes=VMEM_LIMIT)


def _rms(x, gain):
    ms = jnp.mean(x * x, axis=-1, keepdims=True)
    return x * lax.rsqrt(ms + EPS) * gain


def _sigmoid(x):
    return 1.0 / (1.0 + jnp.exp(-x))


def _dot(a, b):
    return jnp.dot(a, b, preferred_element_type=F32)


def _dot_nt(a, b):
    return lax.dot_general(a, b, (((1,), (1,)), ((), ())), preferred_element_type=F32)


def _lane_tile(x, reps):
    return jnp.concatenate([x] * reps, axis=1)


def _dot_tn(a, b):
    return lax.dot_general(a, b, (((0,), (0,)), ((), ())), preferred_element_type=F32)


def _norm_proj_kernel(x_ref, g_ref, w_ref, o_ref, xn_ref):
    @pl.when(pl.program_id(1) == 0)
    def _():
        xn_ref[...] = _rms(x_ref[...], g_ref[...]).astype(BF16)

    o_ref[...] = _dot(xn_ref[...], w_ref[...]).astype(o_ref.dtype)


def _norm_proj(x2d, gain, w, *, tm, tn, out_dtype):
    m, d = x2d.shape
    n = w.shape[1]
    return pl.pallas_call(
        _norm_proj_kernel,
        grid=(m // tm, n // tn),
        in_specs=[pl.BlockSpec((tm, d), lambda i, j: (i, 0)),
                  pl.BlockSpec((1, d), lambda i, j: (0, 0)),
                  pl.BlockSpec((d, tn), lambda i, j: (0, j))],
        out_specs=pl.BlockSpec((tm, tn), lambda i, j: (i, j)),
        out_shape=jax.ShapeDtypeStruct((m, n), out_dtype),
        scratch_shapes=[pltpu.VMEM((tm, d), BF16)],
        compiler_params=_cparams(("arbitrary", "arbitrary")),
        name="even_in_proj",
    )(x2d, gain.reshape(1, d), w)


def _hgrn_consts(c):
    levels = int(np.log2(c))
    tri = np.tril(np.ones((c, c), np.float32))
    mats = [tri]
    masks = []
    t = np.arange(c)
    for l in range(levels):
        h = c >> (l + 1)
        base = (t // (2 * h)) * (2 * h)
        mid = base + h - 1
        mats.append(tri[mid])
        second = (t % (2 * h)) >= h
        same = base[:, None] == base[None, :]
        masks.append((same & second[:, None] & (~second)[None, :]).astype(np.float32))
    masks.append(np.eye(c, dtype=np.float32))
    return np.concatenate(mats, 0), np.stack(masks, 0), levels


def _hgrn_kernel(layer, c, levels, q_ref, f_ref, i_ref, g_ref, lbl_ref, an_ref, sel_ref, msk_ref,
                 o_ref, st_ref):
    @pl.when(pl.program_id(1) == 0)
    def _():
        st_ref[...] = jnp.zeros_like(st_ref)

    lg = lbl_ref[...]
    e = jnp.exp(lg - jnp.max(lg, axis=0, keepdims=True))
    p = e / jnp.sum(e, axis=0, keepdims=True)
    lb = jnp.sum(p[:layer + 1], axis=0, keepdims=True)
    an = an_ref[...]
    n_chunks = q_ref.shape[0] // c
    width = A_HEADS * A_D

    def chunk(ci, carry):
        r0 = pl.multiple_of(ci * c, c)
        qr = q_ref[pl.ds(r0, c), :].astype(F32)
        fr = f_ref[pl.ds(r0, c), :].astype(F32)
        v = i_ref[pl.ds(r0, c), :].astype(BF16)
        gr = g_ref[pl.ds(r0, c), :].astype(F32)
        q = qr * _sigmoid(qr)
        f = lb + (1.0 - lb) * _sigmoid(fr)
        g = jnp.log(f)
        k = 1.0 - f
        g1 = g.astype(BF16)
        r1 = g - g1.astype(F32)
        g2 = r1.astype(BF16)
        g3 = (r1 - g2.astype(F32)).astype(BF16)
        rs = _dot(sel_ref[...], jnp.concatenate([g1, g2, g3], axis=1))
        rs = rs[:, :width] + rs[:, width:2 * width] + rs[:, 2 * width:]
        b = rs[:c]
        hs = [slice(h * A_D, (h + 1) * A_D) for h in range(A_HEADS)]
        qb_, kb_ = q.astype(BF16), k.astype(BF16)
        scores = [msk_ref[levels] * _dot_nt(qb_[:, s_], kb_[:, s_]) for s_ in hs]
        for l in range(levels):
            x = jnp.exp(-jnp.abs(b - rs[(l + 1) * c:(l + 2) * c]))
            qx, kx = (q * x).astype(BF16), (k * x).astype(BF16)
            scores = [sc + msk_ref[l] * _dot_nt(qx[:, s_], kx[:, s_]) for sc, s_ in zip(scores, hs)]
        qe = (q * jnp.exp(b)).astype(BF16)
        b_last = b[c - 1:c, :]
        kd = (k * jnp.exp(b_last - b)).astype(BF16)
        decay = jnp.exp(b_last)
        gate = gr * _sigmoid(gr)
        for h, s_ in enumerate(hs):
            st = st_ref[h]
            o = _dot_nt(qe[:, s_], st.astype(BF16)) + _dot(scores[h].astype(BF16), v[:, s_])
            st_ref[h] = decay[:, s_] * st + _dot_tn(v[:, s_], kd[:, s_])
            o_ref[pl.ds(r0, c), s_] = (_rms(o, an[:, s_]) * gate[:, s_]).astype(o_ref.dtype)
        return carry

    lax.fori_loop(0, n_chunks, chunk, 0)


def _hgrn(z3, lb_logits, a_norm, layer, *, t_rows):
    bsz, s, _ = z3.shape
    sel, msk, levels = _hgrn_consts(CHUNK)
    sel = jnp.asarray(sel, BF16)
    msk = jnp.asarray(msk, F32)
    nl = lb_logits.shape[0]
    width = A_HEADS * A_D
    col = Z_A // width

    def zspec(k):
        return pl.BlockSpec((None, t_rows, width), lambda b, t: (b, t, col + k))

    return pl.pallas_call(
        functools.partial(_hgrn_kernel, layer, CHUNK, levels),
        grid=(bsz, s // t_rows),
        in_specs=[zspec(0), zspec(1), zspec(2), zspec(3),
                  pl.BlockSpec((nl, width), lambda b, t: (0, 0)),
                  pl.BlockSpec((1, width), lambda b, t: (0, 0)),
                  pl.BlockSpec(sel.shape, lambda b, t: (0, 0)),
                  pl.BlockSpec(msk.shape, lambda b, t: (0, 0, 0))],
        out_specs=pl.BlockSpec((None, t_rows, width), lambda b, t: (b, t, 0)),
        out_shape=jax.ShapeDtypeStruct((bsz, s, width), BF16),
        scratch_shapes=[pltpu.VMEM((A_HEADS, A_D, A_D), F32)],
        compiler_params=_cparams(("arbitrary", "arbitrary")),
        name="hgrn2",
    )(z3, z3, z3, z3, lb_logits, a_norm.reshape(1, -1), sel, msk)


INT_MIN = -2 ** 31
KEY_MASK = 0x7FFFFFFF
N_PAIR = B_HEADS // 2
MASKED = -1e30
M_INIT = -1e28
COUNT_ROWS = 128


def _sort_key(x):
    bits = pltpu.bitcast(x, jnp.int32)
    key = bits ^ ((bits >> 31) & KEY_MASK)
    return jnp.where(x == 0.0, 0, key)


def _dsa_kernel(topk, tk, zq_ref, zi_ref, zs_ref, qg_ref, kg_ref, bd_ref, tri_ref, ones_ref, o_ref,
                kibd_ref, knbd_ref, vbd_ref, key_ref, qa_ref, qi_ref, wb_ref, p_ref):
    qb = pl.program_id(1)
    qrows = zq_ref.shape[0]
    n_tiles_all = kibd_ref.shape[0]
    lane = lax.broadcasted_iota(jnp.int32, (1, LANES), 1)
    lo_half = lane < B_HD
    reps = tk // LANES

    @pl.when(qb == 0)
    def _():
        kg = kg_ref[...]
        top = lax.broadcasted_iota(jnp.int32, (LANES, 1), 0) < B_HD

        def both_halves(x):
            return jnp.where(lo_half, x, pltpu.roll(x, B_HD, axis=1))

        def prep(t, carry):
            r0 = pl.multiple_of(t * tk, tk)
            rows = zs_ref[pl.ds(r0, tk), :].astype(F32)
            kv = rows[:, :LANES]
            ms = jnp.sum(jnp.where(lo_half, kv * kv, 0.0), axis=-1, keepdims=True) * (1.0 / B_HD)
            knt = both_halves(kv * lax.rsqrt(ms + EPS) * kg).T
            knbd_ref[t, :, :tk] = jnp.where(top, knt, 0.0).astype(BF16)
            knbd_ref[t, :, tk:] = jnp.where(top, 0.0, knt).astype(BF16)
            kit = both_halves(rows[:, LANES:]).T
            kibd_ref[t, :, :tk] = jnp.where(top, kit, 0.0).astype(BF16)
            kibd_ref[t, :, tk:] = jnp.where(top, 0.0, kit).astype(BF16)
            va = jnp.where(lo_half, pltpu.roll(kv, B_HD, axis=1), jnp.where(lane == B_HD, 1.0, 0.0))
            vb = jnp.where(lo_half, jnp.where(lane == 0, 1.0, 0.0), kv)
            zero = jnp.zeros((tk, LANES), BF16)
            vbd_ref[t, :tk, :LANES] = va.astype(BF16)
            vbd_ref[t, :tk, LANES:] = zero
            vbd_ref[t, tk:, :LANES] = zero
            vbd_ref[t, tk:, LANES:] = vb.astype(BF16)
            return carry

        lax.fori_loop(0, n_tiles_all, prep, 0)

    q = zq_ref[...].astype(F32)
    sq = q * q
    sq_hi = sq.astype(BF16)
    sq_lo = (sq - sq_hi.astype(F32)).astype(BF16)
    ssum = _dot(sq_hi, bd_ref[...]) + _dot(sq_lo, bd_ref[...])
    qn = (q * lax.rsqrt(ssum * (1.0 / B_HD) + EPS) * qg_ref[...]).astype(BF16)
    qi = zi_ref[...].astype(BF16)
    for j in range(N_PAIR):
        qa_ref[j * qrows:(j + 1) * qrows, :] = qn[:, j * LANES:(j + 1) * LANES]
        qi_ref[j * qrows:(j + 1) * qrows, :] = qi[:, j * LANES:(j + 1) * LANES]
    q0 = pl.multiple_of(qb * qrows, qrows)
    w = zs_ref[pl.ds(q0, qrows), :].astype(F32)[:, 3 * B_HD:3 * B_HD + IDX_HEADS]
    w = w * ((IDX_HEADS ** -0.5) * (IDX_DIM ** -0.5))
    for h in range(IDX_HEADS):
        wb_ref[h] = jnp.broadcast_to(w[:, h:h + 1], (qrows, LANES))

    n_tiles = (q0 + qrows + tk - 1) // tk
    row = lax.broadcasted_iota(jnp.int32, (qrows, 1), 0)
    limit = q0 + ((row // CHUNK) + 1) * CHUNK

    def admissible(t):
        col = t * tk + lax.broadcasted_iota(jnp.int32, (qrows, tk), 1)
        return col < limit

    def head_block(x, h):
        j, half = divmod(h, 2)
        return x[j * qrows:(j + 1) * qrows, half * tk:(half + 1) * tk]

    def index_tile(t, carry):
        d = _dot(qi_ref[...], kibd_ref[t])
        acc = jnp.zeros((qrows, tk), F32)
        for h in range(IDX_HEADS):
            wh = _lane_tile(wb_ref[h], reps)
            acc = acc + jnp.maximum(head_block(d, h), 0.0) * wh
        key_ref[t] = jnp.where(admissible(t), _sort_key(acc), INT_MIN)
        return carry

    lax.fori_loop(0, n_tiles, index_tile, 0)

    def count_ge(cand):
        accs = []
        for r0 in range(0, qrows, COUNT_ROWS):
            cr = cand[r0:r0 + COUNT_ROWS]

            def body(t, acc, r0=r0, cr=cr):
                for c in range(reps):
                    keys = key_ref[t, r0:r0 + COUNT_ROWS, c * LANES:(c + 1) * LANES]
                    acc = acc + jnp.where(keys >= cr, 1.0, 0.0)
                return acc
            accs.append(lax.fori_loop(0, n_tiles, body, jnp.zeros((COUNT_ROWS, LANES), F32)))
        return _dot(jnp.concatenate(accs, axis=0).astype(BF16), ones_ref[...])

    def bit_step(i, thr):
        cand = thr + lax.shift_left(jnp.int32(1), 31 - i)
        return jnp.where(count_ge(cand) >= float(topk), cand, thr)

    thr = lax.fori_loop(0, 32, bit_step, jnp.full((qrows, LANES), INT_MIN, jnp.int32))
    need = float(topk) - count_ge(thr + 1)
    need = jnp.where(thr == INT_MIN, 0.0, need)
    need_t = _lane_tile(need, reps)
    thr_t = _lane_tile(thr, reps)

    def attend(t, carry):
        run, m, acc = carry
        keys = key_ref[t]
        eq = keys == thr_t
        eqf = jnp.where(eq, 1.0, 0.0)
        rank = _dot(eqf.astype(BF16), tri_ref[...]) + run
        run = run + jnp.sum(eqf, axis=-1, keepdims=True)
        tie_ok = jnp.where(eq, rank, float(topk)) < need_t
        bias = jnp.where(keys > thr_t, 0.0, jnp.where(tie_ok, 0.0, MASKED))
        s_all = _dot(qa_ref[...], knbd_ref[t])
        m_new, alpha = [], []
        for h in range(B_HEADS):
            j, half = divmod(h, 2)
            s = head_block(s_all, h) + bias
            mh = jnp.maximum(m[h], jnp.max(s, axis=-1, keepdims=True))
            p_ref[j * qrows:(j + 1) * qrows, half * tk:(half + 1) * tk] = jnp.exp(s - mh).astype(BF16)
            alpha.append(jnp.exp(m[h] - mh))
            m_new.append(mh)
        pv = _dot(p_ref[...], vbd_ref[t])
        acc_new = []
        for j in range(N_PAIR):
            al = jnp.concatenate([jnp.broadcast_to(alpha[2 * j], (qrows, LANES)),
                                  jnp.broadcast_to(alpha[2 * j + 1], (qrows, LANES))], axis=1)
            acc_new.append(al * acc[j] + pv[j * qrows:(j + 1) * qrows])
        return run, tuple(m_new), tuple(acc_new)

    init = (jnp.zeros((qrows, 1), F32),
            tuple(jnp.full((qrows, 1), M_INIT, F32) for _ in range(B_HEADS)),
            tuple(jnp.zeros((qrows, 2 * LANES), F32) for _ in range(N_PAIR)))
    _, _, acc = lax.fori_loop(0, n_tiles, attend, init)
    for j in range(N_PAIR):
        a = acc[j]
        o = jnp.where(lo_half, a[:, :LANES] / a[:, B_HD:B_HD + 1], a[:, LANES:] / a[:, LANES:LANES + 1])
        o_ref[:, j * LANES:(j + 1) * LANES] = o.astype(o_ref.dtype)


def _dsa(z3, q_gain, k_gain, *, tk, qrows):
    bsz, s, _ = z3.shape
    topk = min(TOPK_MAX, s // 4)
    assert tk >= topk >= 2 and s % tk == 0 and tk % LANES == 0 and s % qrows == 0 and qrows % CHUNK == 0
    n_t = s // tk
    w_q = B_HEADS * B_HD
    tri = jnp.asarray(np.triu(np.ones((tk, tk), np.float32), 1), BF16)
    head_of = np.arange(w_q) // B_HD
    bd = jnp.asarray(head_of[:, None] == head_of[None, :], BF16)
    ones = jnp.ones((LANES, LANES), BF16)
    assert s // LANES <= 256 and qrows % COUNT_ROWS == 0
    kg = jnp.concatenate([k_gain.astype(F32), jnp.ones((LANES - B_HD,), F32)]).reshape(1, LANES)
    qg = jnp.tile(q_gain.astype(F32) * (B_HD ** -0.5), B_HEADS).reshape(1, w_q)

    def const(a):
        return pl.BlockSpec(a.shape, lambda b, i: (0,) * a.ndim)

    return pl.pallas_call(
        functools.partial(_dsa_kernel, topk, tk),
        grid=(bsz, s // qrows),
        in_specs=[pl.BlockSpec((None, qrows, w_q), lambda b, i: (b, i, Z_BQ // w_q)),
                  pl.BlockSpec((None, qrows, w_q), lambda b, i: (b, i, Z_IQ // w_q)),
                  pl.BlockSpec((None, s, SM_W), lambda b, i: (b, 0, Z_SM // SM_W)),
                  const(qg), const(kg), const(bd), const(tri), const(ones)],
        out_specs=pl.BlockSpec((None, qrows, w_q), lambda b, i: (b, i, 0)),
        out_shape=jax.ShapeDtypeStruct((bsz, s, w_q), BF16),
        scratch_shapes=[pltpu.VMEM((n_t, LANES, 2 * tk), BF16),
                        pltpu.VMEM((n_t, LANES, 2 * tk), BF16),
                        pltpu.VMEM((n_t, 2 * tk, 2 * LANES), BF16),
                        pltpu.VMEM((n_t, qrows, tk), jnp.int32),
                        pltpu.VMEM((N_PAIR * qrows, LANES), BF16),
                        pltpu.VMEM((N_PAIR * qrows, LANES), BF16),
                        pltpu.VMEM((IDX_HEADS, qrows, LANES), F32),
                        pltpu.VMEM((N_PAIR * qrows, 2 * tk), BF16)],
        compiler_params=_cparams(("arbitrary", "arbitrary")),
        name="dsa",
    )(z3, z3, z3, qg, kg, bd, tri, ones)


def _out_proj_kernel(x_ref, a_ref, b_ref, wa_ref, wb_ref, o_ref):
    o_ref[...] = x_ref[...] + _dot(a_ref[...], wa_ref[...]) + _dot(b_ref[...], wb_ref[...])


def _out_proj(x2d, a2d, b2d, w, *, tm, tn):
    m, d = x2d.shape
    ka = a2d.shape[1]
    kb = b2d.shape[1]
    return pl.pallas_call(
        _out_proj_kernel,
        grid=(m // tm, d // tn),
        in_specs=[pl.BlockSpec((tm, tn), lambda i, j: (i, j)),
                  pl.BlockSpec((tm, ka), lambda i, j: (i, 0)),
                  pl.BlockSpec((tm, kb), lambda i, j: (i, 0)),
                  pl.BlockSpec((ka, tn), lambda i, j: (0, j)),
                  pl.BlockSpec((kb, tn), lambda i, j: (0, j))],
        out_specs=pl.BlockSpec((tm, tn), lambda i, j: (i, j)),
        out_shape=jax.ShapeDtypeStruct((m, d), F32),
        compiler_params=_cparams(("arbitrary", "arbitrary")),
        name="even_out_proj",
    )(x2d, a2d, b2d, w[:ka], w[ka:])


def _causal_conv(u, prev, w_ref, b):
    width = w_ref.shape[0]
    rows = u.shape[0]

    def taps(x):
        y = w_ref[width - 1:width, :] * x + b
        for i in range(width - 1):
            y = y + w_ref[i:i + 1, :] * pltpu.roll(x, width - 1 - i, axis=0)
        return y

    head = taps(jnp.concatenate([prev, u[:SUBLANES]], axis=0))[SUBLANES:]
    return jnp.concatenate([head, taps(u)[SUBLANES:]], axis=0) if rows > SUBLANES else head


def _ffn_kernel(x_ref, g_ref, wg_ref, wv_ref, cwg_ref, cwv_ref, cbg_ref, cbv_ref, wd_ref, o_ref,
                xn_ref, acc_ref, pg_ref, pv_ref):
    s = pl.program_id(1)
    j = pl.program_id(2)
    rows = x_ref.shape[0]

    @pl.when(j == 0)
    def _():
        xn_ref[...] = _rms(x_ref[...], g_ref[...]).astype(BF16)
        acc_ref[...] = jnp.zeros_like(acc_ref)

    @pl.when(s == 0)
    def _():
        pg_ref[j] = jnp.zeros(pg_ref.shape[1:], F32)
        pv_ref[j] = jnp.zeros(pv_ref.shape[1:], F32)

    xn = xn_ref[...]

    def branch(w_ref, cw_ref, cb_ref, prev_ref):
        u = _dot(xn, w_ref[...])
        prev = prev_ref[j]
        prev_ref[j] = u[rows - SUBLANES:]
        return _causal_conv(u, prev, cw_ref, cb_ref[...])

    gate = branch(wg_ref, cwg_ref, cbg_ref, pg_ref)
    val = branch(wv_ref, cwv_ref, cbv_ref, pv_ref)
    act = jax.nn.gelu(gate, approximate=True) * val
    acc_ref[...] += _dot(act.astype(BF16), wd_ref[...])

    @pl.when(j == pl.num_programs(2) - 1)
    def _():
        o_ref[...] = x_ref[...] + acc_ref[...]


def _conv_ffn(h3, gain, w_up, conv_w, conv_b, w_down, *, ts, tf):
    bsz, s, d = h3.shape
    dff = w_down.shape[0]
    nj = dff // tf
    width = conv_w.shape[0]
    cb = conv_b.reshape(1, -1)
    xspec = pl.BlockSpec((None, ts, d), lambda b, i, j: (b, i, 0))
    return pl.pallas_call(
        _ffn_kernel,
        grid=(bsz, s // ts, nj),
        in_specs=[xspec,
                  pl.BlockSpec((1, d), lambda b, i, j: (0, 0)),
                  pl.BlockSpec((d, tf), lambda b, i, j: (0, j)),
                  pl.BlockSpec((d, tf), lambda b, i, j: (0, j + nj)),
                  pl.BlockSpec((width, tf), lambda b, i, j: (0, j)),
                  pl.BlockSpec((width, tf), lambda b, i, j: (0, j + nj)),
                  pl.BlockSpec((1, tf), lambda b, i, j: (0, j)),
                  pl.BlockSpec((1, tf), lambda b, i, j: (0, j + nj)),
                  pl.BlockSpec((tf, d), lambda b, i, j: (j, 0))],
        out_specs=xspec,
        out_shape=jax.ShapeDtypeStruct((bsz, s, d), F32),
        scratch_shapes=[pltpu.VMEM((ts, d), BF16),
                        pltpu.VMEM((ts, d), F32),
                        pltpu.VMEM((nj, SUBLANES, tf), F32),
                        pltpu.VMEM((nj, SUBLANES, tf), F32)],
        compiler_params=_cparams(("arbitrary", "arbitrary", "arbitrary")),
        name="conv_ffn",
    )(h3, gain.reshape(1, d), w_up, w_up, conv_w, conv_w, cb, cb, w_down)


def _rglru_kernel(x_ref, g_ref, win_ref, cw_ref, cb_ref, wg_ref, ba_ref, bx_ref, lam_ref,
                  wout_ref, o_ref, a_ref, u_ref, y_ref, px_ref, ph_ref):
    rows = x_ref.shape[0]
    width = LRU_BLOCKS * LRU_BW

    @pl.when(pl.program_id(1) == 0)
    def _():
        px_ref[...] = jnp.zeros_like(px_ref)
        ph_ref[...] = jnp.zeros_like(ph_ref)

    xn = _rms(x_ref[...], g_ref[...]).astype(BF16)
    lam = lam_ref[...]
    log_sig = jnp.minimum(lam, 0.0) - jnp.log1p(jnp.exp(-jnp.abs(lam)))
    pair = 2 * LRU_BW
    for n2 in range(LRU_BLOCKS // 2):
        cs2 = slice(n2 * pair, (n2 + 1) * pair)
        y_ref[:, cs2] = jax.nn.gelu(_dot(xn, win_ref[:, cs2]), approximate=True)
        xb = _dot(xn, win_ref[:, width + n2 * pair:width + (n2 + 1) * pair])
        prev = px_ref[:, cs2]
        px_ref[:, cs2] = xb[rows - SUBLANES:]
        xc2 = _causal_conv(xb, prev, cw_ref.at[:, cs2], cb_ref[:, cs2])
        for n in (2 * n2, 2 * n2 + 1):
            cs = slice(n * LRU_BW, (n + 1) * LRU_BW)
            xc = xc2[:, (n - 2 * n2) * LRU_BW:(n - 2 * n2 + 1) * LRU_BW]
            rg = _dot(xc.astype(BF16), wg_ref[n])
            r = _sigmoid(rg[:, :LRU_BW] + ba_ref[:, cs])
            gi = _sigmoid(rg[:, LRU_BW:] + bx_ref[:, cs])
            log_a = LRU_C * r * log_sig[:, cs]
            a = jnp.exp(log_a)
            a_ref[:, cs] = a
            u_ref[:, cs] = jnp.sqrt(-jnp.tanh(log_a) * (a * a + 1.0)) * (gi * xc)

    sub = lax.broadcasted_iota(jnp.int32, (SUBLANES, 1), 0)

    def group(gi_, h):
        r0 = pl.multiple_of(gi_ * SUBLANES, SUBLANES)
        a = a_ref[pl.ds(r0, SUBLANES), :]
        b = u_ref[pl.ds(r0, SUBLANES), :]
        d = 1
        while d < SUBLANES:
            keep = sub >= d
            b = jnp.where(keep, a * pltpu.roll(b, d, axis=0) + b, b)
            a = jnp.where(keep, a * pltpu.roll(a, d, axis=0), a)
            d *= 2
        hs = a * h + b
        u_ref[pl.ds(r0, SUBLANES), :] = hs
        return hs[SUBLANES - 1:, :]

    ph_ref[0:1, :] = lax.fori_loop(0, rows // SUBLANES, group, ph_ref[0:1, :], unroll=2)
    o_ref[...] = x_ref[...] + _dot((u_ref[...] * y_ref[...]).astype(BF16), wout_ref[...])


def _rglru(h3, gain, w_in, conv_w, conv_b, wa, ba, wx, bx, lam, w_out, *, ts):
    bsz, s, d = h3.shape
    width = LRU_BLOCKS * LRU_BW
    xspec = pl.BlockSpec((None, ts, d), lambda b, i: (b, i, 0))

    def full(a):
        return pl.BlockSpec(a.shape, lambda b, i: (0,) * a.ndim)

    wg = jnp.concatenate([wa, wx], axis=-1)
    args = (gain.reshape(1, d), w_in, conv_w, conv_b.reshape(1, width), wg, ba.reshape(1, width),
            bx.reshape(1, width), lam.reshape(1, width), w_out)
    return pl.pallas_call(
        _rglru_kernel,
        grid=(bsz, s // ts),
        in_specs=[xspec] + [full(a) for a in args],
        out_specs=xspec,
        out_shape=jax.ShapeDtypeStruct((bsz, s, d), F32),
        scratch_shapes=[pltpu.VMEM((ts, width), F32),
                        pltpu.VMEM((ts, width), F32),
                        pltpu.VMEM((ts, width), F32),
                        pltpu.VMEM((SUBLANES, width), F32),
                        pltpu.VMEM((SUBLANES, width), F32)],
        compiler_params=_cparams(("arbitrary", "arbitrary")),
        name="rglru_block",
    )(h3, *args)


def _tile(n, pref):
    t = min(n, pref)
    assert n % t == 0
    return t


def kernel(x, lb_logits, even_norm, even_w_in, even_w_out, a_out_norm, b_q_norm, b_k_norm, odd_norm, odd_w_in, odd_conv_w, odd_conv_b, odd_gate_a_w, odd_gate_a_b, odd_gate_x_w, odd_gate_x_b, odd_lambda, odd_w_out, ffn_norm, ffn_w_up, ffn_conv_w, ffn_conv_b, ffn_w_down):
    bsz, s, d = x.shape
    depth = ffn_norm.shape[0]
    h = x
    for layer in range(depth):
        j = layer // 2
        if layer % 2 == 0:
            w = even_w_in[j]
            o_bk = 2560
            o_iq = o_bk + 2 * B_HD
            o_ik = o_iq + IDX_HEADS * IDX_DIM
            o_end = o_ik + IDX_DIM + IDX_HEADS
            w = jnp.concatenate(
                [w[:, :o_bk], w[:, o_iq:o_ik], w[:, o_bk:o_iq], w[:, o_ik:o_end],
                 jnp.zeros((d, Z_W - o_end), w.dtype)], axis=1).astype(BF16)
            z = _norm_proj(h.reshape(bsz * s, d), even_norm[j], w,
                           tm=_tile(bsz * s, 512), tn=_tile(Z_W, 1664), out_dtype=BF16)
            z3 = z.reshape(bsz, s, Z_W)
            a_o = _hgrn(z3, lb_logits, a_out_norm[j], layer, t_rows=_tile(s, 512))
            b_o = _dsa(z3, b_q_norm[j], b_k_norm[j], tk=_tile(s, 512), qrows=_tile(s, 256))
            h = _out_proj(h.reshape(bsz * s, d), a_o.reshape(bsz * s, -1), b_o.reshape(bsz * s, -1),
                          even_w_out[j].astype(BF16), tm=_tile(bsz * s, 512), tn=d).reshape(bsz, s, d)
        else:
            h = _rglru(h, odd_norm[j], odd_w_in[j].astype(BF16), odd_conv_w[j], odd_conv_b[j],
                       odd_gate_a_w[j].astype(BF16), odd_gate_a_b[j], odd_gate_x_w[j].astype(BF16),
                       odd_gate_x_b[j], odd_lambda[j], odd_w_out[j].astype(BF16), ts=_tile(s, 512))
        h = _conv_ffn(h, ffn_norm[layer], ffn_w_up[layer].astype(BF16), ffn_conv_w[layer],
                      ffn_conv_b[layer], ffn_w_down[layer].astype(BF16), ts=_tile(s, 1024), tf=512)
    return h
```

```python
import functools

import numpy as np
import jax
import jax.numpy as jnp
from jax import lax
from jax.experimental import pallas as pl
from jax.experimental.pallas import tpu as pltpu

F32 = jnp.float32
BF16 = jnp.bfloat16

EPS = 1e-6
CHUNK = 64
A_HEADS = 4
A_D = 128
B_HEADS = 8
B_HD = 64
IDX_HEADS = 8
IDX_DIM = 64
TOPK_MAX = 256
LRU_BLOCKS = 10
LRU_BW = 128
LRU_C = 8.0
LANES = 128
SUBLANES = 8
VMEM_LIMIT = 56 * 1024 * 1024

Z_A = 0
Z_BQ = 2048
Z_IQ = 2560
Z_SM = 3072
Z_W = 3328
SM_W = 256


def _cparams(sem):
    return pltpu.CompilerParams(dimension_semantics=sem, vmem_limit_bytes=VMEM_LIMIT)


def _rms(x, gain):
    ms = jnp.mean(x * x, axis=-1, keepdims=True)
    return x * lax.rsqrt(ms + EPS) * gain


def _sigmoid(x):
    return 1.0 / (1.0 + jnp.exp(-x))


def _dot(a, b):
    return jnp.dot(a, b, preferred_element_type=F32)


def _dot_nt(a, b):
    return lax.dot_general(a, b, (((1,), (1,)), ((), ())), preferred_element_type=F32)


def _lane_tile(x, reps):
    return jnp.concatenate([x] * reps, axis=1)


def _dot_tn(a, b):
    return lax.dot_general(a, b, (((0,), (0,)), ((), ())), preferred_element_type=F32)


def _norm_proj_kernel(x_ref, g_ref, w_ref, o_ref, xn_ref):
    @pl.when(pl.program_id(1) == 0)
    def _():
        xn_ref[...] = _rms(x_ref[...], g_ref[...]).astype(BF16)

    o_ref[...] = _dot(xn_ref[...], w_ref[...]).astype(o_ref.dtype)


def _norm_proj(x2d, gain, w, *, tm, tn, out_dtype):
    m, d = x2d.shape
    n = w.shape[1]
    return pl.pallas_call(
        _norm_proj_kernel,
        grid=(m // tm, n // tn),
        in_specs=[pl.BlockSpec((tm, d), lambda i, j: (i, 0)),
                  pl.BlockSpec((1, d), lambda i, j: (0, 0)),
                  pl.BlockSpec((d, tn), lambda i, j: (0, j))],
        out_specs=pl.BlockSpec((tm, tn), lambda i, j: (i, j)),
        out_shape=jax.ShapeDtypeStruct((m, n), out_dtype),
        scratch_shapes=[pltpu.VMEM((tm, d), BF16)],
        compiler_params=_cparams(("arbitrary", "arbitrary")),
        name="even_in_proj",
    )(x2d, gain.reshape(1, d), w)


def _hgrn_consts(c):
    levels = int(np.log2(c))
    tri = np.tril(np.ones((c, c), np.float32))
    mats = [tri]
    masks = []
    t = np.arange(c)
    for l in range(levels):
        h = c >> (l + 1)
        base = (t // (2 * h)) * (2 * h)
        mid = base + h - 1
        mats.append(tri[mid])
        second = (t % (2 * h)) >= h
        same = base[:, None] == base[None, :]
        masks.append((same & second[:, None] & (~second)[None, :]).astype(np.float32))
    masks.append(np.eye(c, dtype=np.float32))
    return np.concatenate(mats, 0), np.stack(masks, 0), levels


def _hgrn_kernel(layer, c, levels, q_ref, f_ref, i_ref, g_ref, lbl_ref, an_ref, sel_ref, msk_ref,
                 o_ref, st_ref):
    @pl.when(pl.program_id(1) == 0)
    def _():
        st_ref[...] = jnp.zeros_like(st_ref)

    lg = lbl_ref[...]
    e = jnp.exp(lg - jnp.max(lg, axis=0, keepdims=True))
    p = e / jnp.sum(e, axis=0, keepdims=True)
    lb = jnp.sum(p[:layer + 1], axis=0, keepdims=True)
    an = an_ref[...]
    n_chunks = q_ref.shape[0] // c
    width = A_HEADS * A_D

    def chunk(ci, carry):
        r0 = pl.multiple_of(ci * c, c)
        qr = q_ref[pl.ds(r0, c), :].astype(F32)
        fr = f_ref[pl.ds(r0, c), :].astype(F32)
        v = i_ref[pl.ds(r0, c), :].astype(BF16)
        gr = g_ref[pl.ds(r0, c), :].astype(F32)
        q = qr * _sigmoid(qr)
        f = lb + (1.0 - lb) * _sigmoid(fr)
        g = jnp.log(f)
        k = 1.0 - f
        g1 = g.astype(BF16)
        r1 = g - g1.astype(F32)
        g2 = r1.astype(BF16)
        g3 = (r1 - g2.astype(F32)).astype(BF16)
        rs = _dot(sel_ref[...], jnp.concatenate([g1, g2, g3], axis=1))
        rs = rs[:, :width] + rs[:, width:2 * width] + rs[:, 2 * width:]
        b = rs[:c]
        hs = [slice(h * A_D, (h + 1) * A_D) for h in range(A_HEADS)]
        qb_, kb_ = q.astype(BF16), k.astype(BF16)
        scores = [msk_ref[levels] * _dot_nt(qb_[:, s_], kb_[:, s_]) for s_ in hs]
        for l in range(levels):
            x = jnp.exp(-jnp.abs(b - rs[(l + 1) * c:(l + 2) * c]))
            qx, kx = (q * x).astype(BF16), (k * x).astype(BF16)
            scores = [sc + msk_ref[l] * _dot_nt(qx[:, s_], kx[:, s_]) for sc, s_ in zip(scores, hs)]
        qe = (q * jnp.exp(b)).astype(BF16)
        b_last = b[c - 1:c, :]
        kd = (k * jnp.exp(b_last - b)).astype(BF16)
        decay = jnp.exp(b_last)
        gate = gr * _sigmoid(gr)
        for h, s_ in enumerate(hs):
            st = st_ref[h]
            o = _dot_nt(qe[:, s_], st.astype(BF16)) + _dot(scores[h].astype(BF16), v[:, s_])
            st_ref[h] = decay[:, s_] * st + _dot_tn(v[:, s_], kd[:, s_])
            o_ref[pl.ds(r0, c), s_] = (_rms(o, an[:, s_]) * gate[:, s_]).astype(o_ref.dtype)
        return carry

    lax.fori_loop(0, n_chunks, chunk, 0)


def _hgrn(z3, lb_logits, a_norm, layer, *, t_rows):
    bsz, s, _ = z3.shape
    sel, msk, levels = _hgrn_consts(CHUNK)
    sel = jnp.asarray(sel, BF16)
    msk = jnp.asarray(msk, F32)
    nl = lb_logits.shape[0]
    width = A_HEADS * A_D
    col = Z_A // width

    def zspec(k):
        return pl.BlockSpec((None, t_rows, width), lambda b, t: (b, t, col + k))

    return pl.pallas_call(
        functools.partial(_hgrn_kernel, layer, CHUNK, levels),
        grid=(bsz, s // t_rows),
        in_specs=[zspec(0), zspec(1), zspec(2), zspec(3),
                  pl.BlockSpec((nl, width), lambda b, t: (0, 0)),
                  pl.BlockSpec((1, width), lambda b, t: (0, 0)),
                  pl.BlockSpec(sel.shape, lambda b, t: (0, 0)),
                  pl.BlockSpec(msk.shape, lambda b, t: (0, 0, 0))],
        out_specs=pl.BlockSpec((None, t_rows, width), lambda b, t: (b, t, 0)),
        out_shape=jax.ShapeDtypeStruct((bsz, s, width), BF16),
        scratch_shapes=[pltpu.VMEM((A_HEADS, A_D, A_D), F32)],
        compiler_params=_cparams(("arbitrary", "arbitrary")),
        name="hgrn2",
    )(z3, z3, z3, z3, lb_logits, a_norm.reshape(1, -1), sel, msk)


INT_MIN = -2 ** 31
KEY_MASK = 0x7FFFFFFF
MASKED = -1e30
SHIFT_SAFE = 30.0
ONE_COL = B_HD
COUNT_CHAINS = 4


def _sort_key(x):
    bits = pltpu.bitcast(x, jnp.int32)
    key = bits ^ ((bits >> 31) & KEY_MASK)
    return jnp.where(x == 0.0, 0, key)


def _head_col(h, q):
    return ((h % 2) * (B_HEADS // 2) + h // 2) * q


def _dsa_kernel(topk, tk, zq_ref, zi_ref, zs_ref, qg_ref, kg_ref, bd_ref, low_ref, o_ref,
                kn_ref, ki_ref, va_ref, vb_ref, kmax_ref, key_ref, qat_ref, qit_ref, p_ref, acc_ref):
    qb = pl.program_id(1)
    q = zq_ref.shape[0]
    half = (B_HEADS // 2) * q
    n_tiles_all = kn_ref.shape[0]
    lane = lax.broadcasted_iota(jnp.int32, (1, LANES), 1)
    lo_half = lane < B_HD
    one_col = jnp.where(lane == ONE_COL, 1.0, 0.0)

    @pl.when(qb == 0)
    def _():
        kg = kg_ref[...]
        qat_ref[...] = jnp.zeros_like(qat_ref)
        qit_ref[...] = jnp.zeros_like(qit_ref)

        def prep(t, kmax):
            r0 = pl.multiple_of(t * tk, tk)
            rows = zs_ref[pl.ds(r0, tk), :].astype(F32)
            kv = rows[:, :LANES]
            ms = jnp.sum(jnp.where(lo_half, kv * kv, 0.0), axis=-1, keepdims=True) * (1.0 / B_HD)
            kn = jnp.where(lo_half, kv * lax.rsqrt(ms + EPS) * kg, 0.0)
            kn_ref[t] = (kn + one_col).astype(BF16)
            ki_ref[t] = jnp.where(lo_half, rows[:, LANES:], 0.0).astype(BF16)
            va_ref[t] = jnp.where(lo_half, pltpu.roll(kv, B_HD, axis=1), one_col).astype(BF16)
            vb_ref[t] = jnp.where(lo_half, jnp.where(lane == 0, 1.0, 0.0), kv).astype(BF16)
            return jnp.maximum(kmax, jnp.max(jnp.sum(kn * kn, axis=-1, keepdims=True), axis=0, keepdims=True))

        kmax = lax.fori_loop(0, n_tiles_all, prep, jnp.zeros((1, 1), F32))
        kmax_ref[...] = jnp.broadcast_to(jnp.sqrt(kmax), kmax_ref.shape)

    qf = zq_ref[...].astype(F32)
    sq = qf * qf
    sq_hi = sq.astype(BF16)
    sq_lo = (sq - sq_hi.astype(F32)).astype(BF16)
    ssum = _dot(sq_hi, bd_ref[...]) + _dot(sq_lo, bd_ref[...])
    qnt = (qf * lax.rsqrt(ssum * (1.0 / B_HD) + EPS) * qg_ref[...]).T
    qit = zi_ref[...].astype(F32).T
    kmax_q = _lane_tile(kmax_ref[0:1, :], q // LANES)
    bound = []
    for h in range(B_HEADS):
        c0 = _head_col(h, q)
        blk = qnt[h * B_HD:(h + 1) * B_HD, :]
        qat_ref[0:B_HD, c0:c0 + q] = blk.astype(BF16)
        qit_ref[0:IDX_DIM, c0:c0 + q] = qit[h * IDX_DIM:(h + 1) * IDX_DIM, :].astype(BF16)
        bound.append(jnp.sqrt(jnp.sum(blk * blk, axis=0, keepdims=True)) * kmax_q)
    q0 = pl.multiple_of(qb * q, q)
    wt = zs_ref[pl.ds(q0, q), :].astype(F32)[:, LANES:].T
    wt = wt[IDX_DIM:IDX_DIM + SUBLANES, :] * ((IDX_HEADS ** -0.5) * (IDX_DIM ** -0.5))

    n_tiles = (q0 + q + tk - 1) // tk
    qpos = lax.broadcasted_iota(jnp.int32, (1, q), 1)
    limit = q0 + ((qpos // CHUNK) + 1) * CHUNK

    def index_tile(t, carry):
        d = _dot(ki_ref[t], qit_ref[...])
        acc = jnp.zeros((tk, q), F32)
        for h in range(IDX_HEADS):
            c0 = _head_col(h, q)
            acc = acc + jnp.maximum(d[:, c0:c0 + q], 0.0) * wt[h:h + 1, :]
        kpos = t * tk + lax.broadcasted_iota(jnp.int32, (tk, 1), 0)
        key_ref[t] = jnp.where(kpos < limit, _sort_key(acc), INT_MIN)
        return carry

    lax.fori_loop(0, n_tiles, index_tile, 0)

    def count_ge(cand):
        def body(t, acc):
            hit = jnp.where(key_ref[t] >= cand, 1.0, 0.0)
            return acc + jnp.sum(hit.reshape(COUNT_CHAINS, tk // (COUNT_CHAINS * SUBLANES), SUBLANES, q), axis=1)
        acc = lax.fori_loop(0, n_tiles, body, jnp.zeros((COUNT_CHAINS, SUBLANES, q), F32))
        return jnp.sum(jnp.sum(acc, axis=0), axis=0, keepdims=True)

    def bit_step(i, thr):
        cand = thr + lax.shift_left(jnp.int32(1), 31 - i)
        return jnp.where(count_ge(cand) >= float(topk), cand, thr)

    thr = lax.fori_loop(0, 32, bit_step, jnp.full((1, q), INT_MIN, jnp.int32))
    need = float(topk) - count_ge(thr + 1)
    need = jnp.where(thr == INT_MIN, 0.0, need)

    def bias_tile(t, run):
        keys = key_ref[t]
        eq = keys == thr
        eqf = jnp.where(eq, 1.0, 0.0)
        rank = _dot(low_ref[...], eqf.astype(BF16)) + run
        tie_ok = jnp.where(eq, rank, float(topk)) < need
        bias = jnp.where(keys > thr, 0.0, jnp.where(tie_ok, 0.0, MASKED))
        return bias, run + jnp.sum(eqf, axis=0, keepdims=True)

    run0 = jnp.zeros((1, q), F32)
    bound = jnp.concatenate([bound[h] for h in sorted(range(B_HEADS), key=lambda h: _head_col(h, q))], axis=1)

    def exact_shift():
        def body(t, carry):
            run, mx = carry
            bias, run = bias_tile(t, run)
            s = _dot(kn_ref[t], qat_ref[...])
            cols = [jnp.max(s[:, c0:c0 + q] + bias, axis=0, keepdims=True) for c0 in range(0, 2 * half, q)]
            return run, jnp.maximum(mx, jnp.concatenate(cols, axis=1))
        return lax.fori_loop(0, n_tiles, body, (run0, jnp.full((1, 2 * half), MASKED, F32)))[1]

    shift_rows = lax.broadcasted_iota(jnp.int32, (2 * SUBLANES, 1), 0) == 0
    qat_ref[B_HD:B_HD + 2 * SUBLANES, :] = jnp.zeros((2 * SUBLANES, 2 * half), BF16)
    shift = lax.cond(jnp.max(bound) > SHIFT_SAFE, exact_shift, lambda: bound)
    qat_ref[B_HD:B_HD + 2 * SUBLANES, :] = jnp.where(shift_rows, -shift, 0.0).astype(BF16)

    acc_ref[...] = jnp.zeros_like(acc_ref)

    def attend(t, run):
        bias, run = bias_tile(t, run)
        s = _dot(kn_ref[t], qat_ref[...])
        for c0 in range(0, 2 * half, q):
            p_ref[:, c0:c0 + q] = jnp.exp(s[:, c0:c0 + q] + bias).astype(BF16)
        acc_ref[:half] += _dot_tn(p_ref[:, :half], va_ref[t])
        acc_ref[half:] += _dot_tn(p_ref[:, half:], vb_ref[t])
        return run

    lax.fori_loop(0, n_tiles, attend, run0)
    for j in range(B_HEADS // 2):
        a = acc_ref[j * q:(j + 1) * q]
        b = acc_ref[half + j * q:half + (j + 1) * q]
        o = jnp.where(lo_half, a / a[:, ONE_COL:ONE_COL + 1], b / b[:, 0:1])
        o_ref[:, j * LANES:(j + 1) * LANES] = o.astype(o_ref.dtype)


def _dsa(z3, q_gain, k_gain, *, tk, qrows):
    bsz, s, _ = z3.shape
    topk = min(TOPK_MAX, s // 4)
    assert tk >= topk >= 2 and s % tk == 0 and tk % LANES == 0 and s % qrows == 0 and qrows % LANES == 0
    n_t = s // tk
    w_q = B_HEADS * B_HD
    low = jnp.asarray(np.tril(np.ones((tk, tk), np.float32), -1), BF16)
    head_of = np.arange(w_q) // B_HD
    bd = jnp.asarray(head_of[:, None] == head_of[None, :], BF16)
    kg = jnp.concatenate([k_gain.astype(F32), jnp.ones((LANES - B_HD,), F32)]).reshape(1, LANES)
    qg = jnp.tile(q_gain.astype(F32) * (B_HD ** -0.5), B_HEADS).reshape(1, w_q)

    def const(a):
        return pl.BlockSpec(a.shape, lambda b, i: (0,) * a.ndim)

    return pl.pallas_call(
        functools.partial(_dsa_kernel, topk, tk),
        grid=(bsz, s // qrows),
        in_specs=[pl.BlockSpec((None, qrows, w_q), lambda b, i: (b, i, Z_BQ // w_q)),
                  pl.BlockSpec((None, qrows, w_q), lambda b, i: (b, i, Z_IQ // w_q)),
                  pl.BlockSpec((None, s, SM_W), lambda b, i: (b, 0, Z_SM // SM_W)),
                  const(qg), const(kg), const(bd), const(low)],
        out_specs=pl.BlockSpec((None, qrows, w_q), lambda b, i: (b, i, 0)),
        out_shape=jax.ShapeDtypeStruct((bsz, s, w_q), BF16),
        scratch_shapes=[pltpu.VMEM((n_t, tk, LANES), BF16),
                        pltpu.VMEM((n_t, tk, LANES), BF16),
                        pltpu.VMEM((n_t, tk, LANES), BF16),
                        pltpu.VMEM((n_t, tk, LANES), BF16),
                        pltpu.VMEM((SUBLANES, LANES), F32),
                        pltpu.VMEM((n_t, tk, qrows), jnp.int32),
                        pltpu.VMEM((LANES, B_HEADS * qrows), BF16),
                        pltpu.VMEM((LANES, IDX_HEADS * qrows), BF16),
                        pltpu.VMEM((tk, B_HEADS * qrows), BF16),
                        pltpu.VMEM((B_HEADS * qrows, LANES), F32)],
        compiler_params=_cparams(("arbitrary", "arbitrary")),
        name="dsa",
    )(z3, z3, z3, qg, kg, bd, low)


def _out_proj_kernel(x_ref, a_ref, b_ref, wa_ref, wb_ref, o_ref):
    o_ref[...] = x_ref[...] + _dot(a_ref[...], wa_ref[...]) + _dot(b_ref[...], wb_ref[...])


def _out_proj(x2d, a2d, b2d, w, *, tm, tn):
    m, d = x2d.shape
    ka = a2d.shape[1]
    kb = b2d.shape[1]
    return pl.pallas_call(
        _out_proj_kernel,
        grid=(m // tm, d // tn),
        in_specs=[pl.BlockSpec((tm, tn), lambda i, j: (i, j)),
                  pl.BlockSpec((tm, ka), lambda i, j: (i, 0)),
                  pl.BlockSpec((tm, kb), lambda i, j: (i, 0)),
                  pl.BlockSpec((ka, tn), lambda i, j: (0, j)),
                  pl.BlockSpec((kb, tn), lambda i, j: (0, j))],
        out_specs=pl.BlockSpec((tm, tn), lambda i, j: (i, j)),
        out_shape=jax.ShapeDtypeStruct((m, d), F32),
        compiler_params=_cparams(("arbitrary", "arbitrary")),
        name="even_out_proj",
    )(x2d, a2d, b2d, w[:ka], w[ka:])


def _causal_conv(u, prev, w_ref, b):
    width = w_ref.shape[0]
    rows = u.shape[0]

    def taps(x):
        y = w_ref[width - 1:width, :] * x + b
        for i in range(width - 1):
            y = y + w_ref[i:i + 1, :] * pltpu.roll(x, width - 1 - i, axis=0)
        return y

    head = taps(jnp.concatenate([prev, u[:SUBLANES]], axis=0))[SUBLANES:]
    return jnp.concatenate([head, taps(u)[SUBLANES:]], axis=0) if rows > SUBLANES else head


def _ffn_kernel(x_ref, g_ref, wg_ref, wv_ref, cwg_ref, cwv_ref, cbg_ref, cbv_ref, wd_ref, o_ref,
                xn_ref, acc_ref, pg_ref, pv_ref):
    s = pl.program_id(1)
    j = pl.program_id(2)
    rows = x_ref.shape[0]

    @pl.when(j == 0)
    def _():
        xn_ref[...] = _rms(x_ref[...], g_ref[...]).astype(BF16)
        acc_ref[...] = jnp.zeros_like(acc_ref)

    @pl.when(s == 0)
    def _():
        pg_ref[j] = jnp.zeros(pg_ref.shape[1:], F32)
        pv_ref[j] = jnp.zeros(pv_ref.shape[1:], F32)

    xn = xn_ref[...]

    def branch(w_ref, cw_ref, cb_ref, prev_ref):
        u = _dot(xn, w_ref[...])
        prev = prev_ref[j]
        prev_ref[j] = u[rows - SUBLANES:]
        return _causal_conv(u, prev, cw_ref, cb_ref[...])

    gate = branch(wg_ref, cwg_ref, cbg_ref, pg_ref)
    val = branch(wv_ref, cwv_ref, cbv_ref, pv_ref)
    act = jax.nn.gelu(gate, approximate=True) * val
    acc_ref[...] += _dot(act.astype(BF16), wd_ref[...])

    @pl.when(j == pl.num_programs(2) - 1)
    def _():
        o_ref[...] = x_ref[...] + acc_ref[...]


def _conv_ffn(h3, gain, w_up, conv_w, conv_b, w_down, *, ts, tf):
    bsz, s, d = h3.shape
    dff = w_down.shape[0]
    nj = dff // tf
    width = conv_w.shape[0]
    cb = conv_b.reshape(1, -1)
    xspec = pl.BlockSpec((None, ts, d), lambda b, i, j: (b, i, 0))
    return pl.pallas_call(
        _ffn_kernel,
        grid=(bsz, s // ts, nj),
        in_specs=[xspec,
                  pl.BlockSpec((1, d), lambda b, i, j: (0, 0)),
                  pl.BlockSpec((d, tf), lambda b, i, j: (0, j)),
                  pl.BlockSpec((d, tf), lambda b, i, j: (0, j + nj)),
                  pl.BlockSpec((width, tf), lambda b, i, j: (0, j)),
                  pl.BlockSpec((width, tf), lambda b, i, j: (0, j + nj)),
                  pl.BlockSpec((1, tf), lambda b, i, j: (0, j)),
                  pl.BlockSpec((1, tf), lambda b, i, j: (0, j + nj)),
                  pl.BlockSpec((tf, d), lambda b, i, j: (j, 0))],
        out_specs=xspec,
        out_shape=jax.ShapeDtypeStruct((bsz, s, d), F32),
        scratch_shapes=[pltpu.VMEM((ts, d), BF16),
                        pltpu.VMEM((ts, d), F32),
                        pltpu.VMEM((nj, SUBLANES, tf), F32),
                        pltpu.VMEM((nj, SUBLANES, tf), F32)],
        compiler_params=_cparams(("arbitrary", "arbitrary", "arbitrary")),
        name="conv_ffn",
    )(h3, gain.reshape(1, d), w_up, w_up, conv_w, conv_w, cb, cb, w_down)


def _rglru_kernel(x_ref, g_ref, win_ref, cw_ref, cb_ref, wg_ref, ba_ref, bx_ref, lam_ref,
                  wout_ref, o_ref, a_ref, u_ref, y_ref, px_ref, ph_ref):
    rows = x_ref.shape[0]
    width = LRU_BLOCKS * LRU_BW

    @pl.when(pl.program_id(1) == 0)
    def _():
        px_ref[...] = jnp.zeros_like(px_ref)
        ph_ref[...] = jnp.zeros_like(ph_ref)

    xn = _rms(x_ref[...], g_ref[...]).astype(BF16)
    lam = lam_ref[...]
    log_sig = jnp.minimum(lam, 0.0) - jnp.log1p(jnp.exp(-jnp.abs(lam)))
    pair = 2 * LRU_BW
    for n2 in range(LRU_BLOCKS // 2):
        cs2 = slice(n2 * pair, (n2 + 1) * pair)
        y_ref[:, cs2] = jax.nn.gelu(_dot(xn, win_ref[:, cs2]), approximate=True)
        xb = _dot(xn, win_ref[:, width + n2 * pair:width + (n2 + 1) * pair])
        prev = px_ref[:, cs2]
        px_ref[:, cs2] = xb[rows - SUBLANES:]
        xc2 = _causal_conv(xb, prev, cw_ref.at[:, cs2], cb_ref[:, cs2])
        for n in (2 * n2, 2 * n2 + 1):
            cs = slice(n * LRU_BW, (n + 1) * LRU_BW)
            xc = xc2[:, (n - 2 * n2) * LRU_BW:(n - 2 * n2 + 1) * LRU_BW]
            rg = _dot(xc.astype(BF16), wg_ref[n])
            r = _sigmoid(rg[:, :LRU_BW] + ba_ref[:, cs])
            gi = _sigmoid(rg[:, LRU_BW:] + bx_ref[:, cs])
            log_a = LRU_C * r * log_sig[:, cs]
            a = jnp.exp(log_a)
            a_ref[:, cs] = a
            u_ref[:, cs] = jnp.sqrt(-jnp.tanh(log_a) * (a * a + 1.0)) * (gi * xc)

    sub = lax.broadcasted_iota(jnp.int32, (SUBLANES, 1), 0)

    def group(gi_, h):
        r0 = pl.multiple_of(gi_ * SUBLANES, SUBLANES)
        a = a_ref[pl.ds(r0, SUBLANES), :]
        b = u_ref[pl.ds(r0, SUBLANES), :]
        d = 1
        while d < SUBLANES:
            keep = sub >= d
            b = jnp.where(keep, a * pltpu.roll(b, d, axis=0) + b, b)
            a = jnp.where(keep, a * pltpu.roll(a, d, axis=0), a)
            d *= 2
        hs = a * h + b
        u_ref[pl.ds(r0, SUBLANES), :] = hs
        return hs[SUBLANES - 1:, :]

    ph_ref[0:1, :] = lax.fori_loop(0, rows // SUBLANES, group, ph_ref[0:1, :], unroll=2)
    o_ref[...] = x_ref[...] + _dot((u_ref[...] * y_ref[...]).astype(BF16), wout_ref[...])


def _rglru(h3, gain, w_in, conv_w, conv_b, wa, ba, wx, bx, lam, w_out, *, ts):
    bsz, s, d = h3.shape
    width = LRU_BLOCKS * LRU_BW
    xspec = pl.BlockSpec((None, ts, d), lambda b, i: (b, i, 0))

    def full(a):
        return pl.BlockSpec(a.shape, lambda b, i: (0,) * a.ndim)

    wg = jnp.concatenate([wa, wx], axis=-1)
    args = (gain.reshape(1, d), w_in, conv_w, conv_b.reshape(1, width), wg, ba.reshape(1, width),
            bx.reshape(1, width), lam.reshape(1, width), w_out)
    return pl.pallas_call(
        _rglru_kernel,
        grid=(bsz, s // ts),
        in_specs=[xspec] + [full(a) for a in args],
        out_specs=xspec,
        out_shape=jax.ShapeDtypeStruct((bsz, s, d), F32),
        scratch_shapes=[pltpu.VMEM((ts, width), F32),
                        pltpu.VMEM((ts, width), F32),
                        pltpu.VMEM((ts, width), F32),
                        pltpu.VMEM((SUBLANES, width), F32),
                        pltpu.VMEM((SUBLANES, width), F32)],
        compiler_params=_cparams(("arbitrary", "arbitrary")),
        name="rglru_block",
    )(h3, *args)


def _tile(n, pref):
    t = min(n, pref)
    assert n % t == 0
    return t


def kernel(x, lb_logits, even_norm, even_w_in, even_w_out, a_out_norm, b_q_norm, b_k_norm, odd_norm, odd_w_in, odd_conv_w, odd_conv_b, odd_gate_a_w, odd_gate_a_b, odd_gate_x_w, odd_gate_x_b, odd_lambda, odd_w_out, ffn_norm, ffn_w_up, ffn_conv_w, ffn_conv_b, ffn_w_down):
    bsz, s, d = x.shape
    depth = ffn_norm.shape[0]
    h = x
    for layer in range(depth):
        j = layer // 2
        if layer % 2 == 0:
            w = even_w_in[j]
            o_bk = 2560
            o_iq = o_bk + 2 * B_HD
            o_ik = o_iq + IDX_HEADS * IDX_DIM
            o_end = o_ik + IDX_DIM + IDX_HEADS
            w = jnp.concatenate(
                [w[:, :o_bk], w[:, o_iq:o_ik], w[:, o_bk:o_iq], w[:, o_ik:o_end],
                 jnp.zeros((d, Z_W - o_end), w.dtype)], axis=1).astype(BF16)
            z = _norm_proj(h.reshape(bsz * s, d), even_norm[j], w,
                           tm=_tile(bsz * s, 512), tn=_tile(Z_W, 1664), out_dtype=BF16)
            z3 = z.reshape(bsz, s, Z_W)
            a_o = _hgrn(z3, lb_logits, a_out_norm[j], layer, t_rows=_tile(s, 512))
            b_o = _dsa(z3, b_q_norm[j], b_k_norm[j], tk=_tile(s, 512), qrows=_tile(s, 256))
            h = _out_proj(h.reshape(bsz * s, d), a_o.reshape(bsz * s, -1), b_o.reshape(bsz * s, -1),
                          even_w_out[j].astype(BF16), tm=_tile(bsz * s, 512), tn=d).reshape(bsz, s, d)
        else:
            h = _rglru(h, odd_norm[j], odd_w_in[j].astype(BF16), odd_conv_w[j], odd_conv_b[j],
                       odd_gate_a_w[j].astype(BF16), odd_gate_a_b[j], odd_gate_x_w[j].astype(BF16),
                       odd_gate_x_b[j], odd_lambda[j], odd_w_out[j].astype(BF16), ts=_tile(s, 512))
        h = _conv_ffn(h, ffn_norm[layer], ffn_w_up[layer].astype(BF16), ffn_conv_w[layer],
                      ffn_conv_b[layer], ffn_w_down[layer].astype(BF16), ts=_tile(s, 1024), tf=512)
    return h
```

```python
import functools

import numpy as np
import jax
import jax.numpy as jnp
from jax import lax
from jax.experimental import pallas as pl
from jax.experimental.pallas import tpu as pltpu

F32 = jnp.float32
BF16 = jnp.bfloat16

EPS = 1e-6
CHUNK = 64
A_HEADS = 4
A_D = 128
B_HEADS = 8
B_HD = 64
IDX_HEADS = 8
IDX_DIM = 64
TOPK_MAX = 256
LRU_BLOCKS = 10
LRU_BW = 128
LRU_C = 8.0
LANES = 128
SUBLANES = 8
VMEM_LIMIT = 56 * 1024 * 1024

Z_A = 0
Z_BQ = 2048
Z_IQ = 2560
Z_SM = 3072
Z_W = 3328
SM_W = 256


def _cparams(sem):
    return pltpu.CompilerParams(dimension_semantics=sem, vmem_limit_bytes=VMEM_LIMIT)


def _rms(x, gain):
    ms = jnp.mean(x * x, axis=-1, keepdims=True)
    return x * lax.rsqrt(ms + EPS) * gain


def _sigmoid(x):
    return 1.0 / (1.0 + jnp.exp(-x))


def _dot(a, b):
    return jnp.dot(a, b, preferred_element_type=F32)


def _dot_nt(a, b):
    return lax.dot_general(a, b, (((1,), (1,)), ((), ())), preferred_element_type=F32)


def _lane_tile(x, reps):
    return jnp.concatenate([x] * reps, axis=1)


def _dot_tn(a, b):
    return lax.dot_general(a, b, (((0,), (0,)), ((), ())), preferred_element_type=F32)


def _norm_proj_kernel(x_ref, g_ref, w_ref, o_ref, xn_ref):
    @pl.when(pl.program_id(1) == 0)
    def _():
        xn_ref[...] = _rms(x_ref[...], g_ref[...]).astype(BF16)

    o_ref[...] = _dot(xn_ref[...], w_ref[...]).astype(o_ref.dtype)


def _norm_proj(x2d, gain, w, *, tm, tn, out_dtype):
    m, d = x2d.shape
    n = w.shape[1]
    return pl.pallas_call(
        _norm_proj_kernel,
        grid=(m // tm, n // tn),
        in_specs=[pl.BlockSpec((tm, d), lambda i, j: (i, 0)),
                  pl.BlockSpec((1, d), lambda i, j: (0, 0)),
                  pl.BlockSpec((d, tn), lambda i, j: (0, j))],
        out_specs=pl.BlockSpec((tm, tn), lambda i, j: (i, j)),
        out_shape=jax.ShapeDtypeStruct((m, n), out_dtype),
        scratch_shapes=[pltpu.VMEM((tm, d), BF16)],
        compiler_params=_cparams(("arbitrary", "arbitrary")),
        name="even_in_proj",
    )(x2d, gain.reshape(1, d), w)


def _hgrn_consts(c):
    levels = int(np.log2(c))
    tri = np.tril(np.ones((c, c), np.float32))
    mats = [tri]
    masks = []
    t = np.arange(c)
    for l in range(levels):
        h = c >> (l + 1)
        base = (t // (2 * h)) * (2 * h)
        mid = base + h - 1
        mats.append(tri[mid])
        second = (t % (2 * h)) >= h
        same = base[:, None] == base[None, :]
        masks.append((same & second[:, None] & (~second)[None, :]).astype(np.float32))
    masks.append(np.eye(c, dtype=np.float32))
    return np.concatenate(mats, 0), np.stack(masks, 0), levels


def _hgrn_kernel(layer, c, levels, q_ref, f_ref, i_ref, g_ref, lbl_ref, an_ref, sel_ref, msk_ref,
                 o_ref, st_ref):
    @pl.when(pl.program_id(1) == 0)
    def _():
        st_ref[...] = jnp.zeros_like(st_ref)

    lg = lbl_ref[...]
    e = jnp.exp(lg - jnp.max(lg, axis=0, keepdims=True))
    p = e / jnp.sum(e, axis=0, keepdims=True)
    lb = jnp.sum(p[:layer + 1], axis=0, keepdims=True)
    an = an_ref[...]
    n_chunks = q_ref.shape[0] // c
    width = A_HEADS * A_D

    def chunk(ci, carry):
        r0 = pl.multiple_of(ci * c, c)
        qr = q_ref[pl.ds(r0, c), :].astype(F32)
        fr = f_ref[pl.ds(r0, c), :].astype(F32)
        v = i_ref[pl.ds(r0, c), :].astype(BF16)
        gr = g_ref[pl.ds(r0, c), :].astype(F32)
        q = qr * _sigmoid(qr)
        f = lb + (1.0 - lb) * _sigmoid(fr)
        g = jnp.log(f)
        k = 1.0 - f
        g1 = g.astype(BF16)
        r1 = g - g1.astype(F32)
        g2 = r1.astype(BF16)
        g3 = (r1 - g2.astype(F32)).astype(BF16)
        rs = _dot(sel_ref[...], jnp.concatenate([g1, g2, g3], axis=1))
        rs = rs[:, :width] + rs[:, width:2 * width] + rs[:, 2 * width:]
        b = rs[:c]
        hs = [slice(h * A_D, (h + 1) * A_D) for h in range(A_HEADS)]
        qb_, kb_ = q.astype(BF16), k.astype(BF16)
        scores = [msk_ref[levels] * _dot_nt(qb_[:, s_], kb_[:, s_]) for s_ in hs]
        for l in range(levels):
            x = jnp.exp(-jnp.abs(b - rs[(l + 1) * c:(l + 2) * c]))
            qx, kx = (q * x).astype(BF16), (k * x).astype(BF16)
            scores = [sc + msk_ref[l] * _dot_nt(qx[:, s_], kx[:, s_]) for sc, s_ in zip(scores, hs)]
        qe = (q * jnp.exp(b)).astype(BF16)
        b_last = b[c - 1:c, :]
        kd = (k * jnp.exp(b_last - b)).astype(BF16)
        decay = jnp.exp(b_last)
        gate = gr * _sigmoid(gr)
        for h, s_ in enumerate(hs):
            st = st_ref[h]
            o = _dot_nt(qe[:, s_], st.astype(BF16)) + _dot(scores[h].astype(BF16), v[:, s_])
            st_ref[h] = decay[:, s_] * st + _dot_tn(v[:, s_], kd[:, s_])
            o_ref[pl.ds(r0, c), s_] = (_rms(o, an[:, s_]) * gate[:, s_]).astype(o_ref.dtype)
        return carry

    lax.fori_loop(0, n_chunks, chunk, 0)


def _hgrn(z3, lb_logits, a_norm, layer, *, t_rows):
    bsz, s, _ = z3.shape
    sel, msk, levels = _hgrn_consts(CHUNK)
    sel = jnp.asarray(sel, BF16)
    msk = jnp.asarray(msk, F32)
    nl = lb_logits.shape[0]
    width = A_HEADS * A_D
    col = Z_A // width

    def zspec(k):
        return pl.BlockSpec((None, t_rows, width), lambda b, t: (b, t, col + k))

    return pl.pallas_call(
        functools.partial(_hgrn_kernel, layer, CHUNK, levels),
        grid=(bsz, s // t_rows),
        in_specs=[zspec(0), zspec(1), zspec(2), zspec(3),
                  pl.BlockSpec((nl, width), lambda b, t: (0, 0)),
                  pl.BlockSpec((1, width), lambda b, t: (0, 0)),
                  pl.BlockSpec(sel.shape, lambda b, t: (0, 0)),
                  pl.BlockSpec(msk.shape, lambda b, t: (0, 0, 0))],
        out_specs=pl.BlockSpec((None, t_rows, width), lambda b, t: (b, t, 0)),
        out_shape=jax.ShapeDtypeStruct((bsz, s, width), BF16),
        scratch_shapes=[pltpu.VMEM((A_HEADS, A_D, A_D), F32)],
        compiler_params=_cparams(("arbitrary", "arbitrary")),
        name="hgrn2",
    )(z3, z3, z3, z3, lb_logits, a_norm.reshape(1, -1), sel, msk)


INT_MIN = -2 ** 31
KEY_MASK = 0x7FFFFFFF
MASKED = -1e30
SHIFT_SAFE = 30.0
ONE_COL = B_HD
COUNT_CHAINS = 4

def _sort_key(x):
    bits = pltpu.bitcast(x, jnp.int32)
    key = bits ^ ((bits >> 31) & KEY_MASK)
    return jnp.where(x == 0.0, 0, key)


def _head_col(h, q):
    return ((h % 2) * (B_HEADS // 2) + h // 2) * q


def _dsa_kernel(topk, tk, zq_ref, zi_ref, zs_ref, qg_ref, kg_ref, bd_ref, low_ref, o_ref,
                kn_ref, ki_ref, va_ref, vb_ref, kmax_ref, key_ref, qat_ref, qit_ref, p_ref, acc_ref):
    qb = pl.program_id(1)
    q = zq_ref.shape[0]
    half = (B_HEADS // 2) * q
    n_tiles_all = kn_ref.shape[0]
    lane = lax.broadcasted_iota(jnp.int32, (1, LANES), 1)
    lo_half = lane < B_HD
    one_col = jnp.where(lane == ONE_COL, 1.0, 0.0)

    @pl.when(qb == 0)
    def _():
        kg = kg_ref[...]
        qat_ref[...] = jnp.zeros_like(qat_ref)
        qit_ref[...] = jnp.zeros_like(qit_ref)

        def prep(t, kmax):
            r0 = pl.multiple_of(t * tk, tk)
            rows = zs_ref[pl.ds(r0, tk), :].astype(F32)
            kv = rows[:, :LANES]
            ms = jnp.sum(jnp.where(lo_half, kv * kv, 0.0), axis=-1, keepdims=True) * (1.0 / B_HD)
            kn = jnp.where(lo_half, kv * lax.rsqrt(ms + EPS) * kg, 0.0)
            kn_ref[t] = (kn + one_col).astype(BF16)
            ki_ref[t] = jnp.where(lo_half, rows[:, LANES:], 0.0).astype(BF16)
            va_ref[t] = jnp.where(lo_half, pltpu.roll(kv, B_HD, axis=1), one_col).astype(BF16)
            vb_ref[t] = jnp.where(lo_half, jnp.where(lane == 0, 1.0, 0.0), kv).astype(BF16)
            return jnp.maximum(kmax, jnp.max(jnp.sum(kn * kn, axis=-1, keepdims=True), axis=0, keepdims=True))

        kmax = lax.fori_loop(0, n_tiles_all, prep, jnp.zeros((1, 1), F32))
        kmax_ref[...] = jnp.broadcast_to(jnp.sqrt(kmax), kmax_ref.shape)

    qf = zq_ref[...].astype(F32)
    sq = qf * qf
    sq_hi = sq.astype(BF16)
    sq_lo = (sq - sq_hi.astype(F32)).astype(BF16)
    ssum = _dot(sq_hi, bd_ref[...]) + _dot(sq_lo, bd_ref[...])
    qnt = (qf * lax.rsqrt(ssum * (1.0 / B_HD) + EPS) * qg_ref[...]).T
    qit = zi_ref[...].astype(F32).T
    kmax_q = _lane_tile(kmax_ref[0:1, :], q // LANES)
    bound = []
    for h in range(B_HEADS):
        c0 = _head_col(h, q)
        blk = qnt[h * B_HD:(h + 1) * B_HD, :]
        qat_ref[0:B_HD, c0:c0 + q] = blk.astype(BF16)
        qit_ref[0:IDX_DIM, c0:c0 + q] = qit[h * IDX_DIM:(h + 1) * IDX_DIM, :].astype(BF16)
        bound.append(jnp.sqrt(jnp.sum(blk * blk, axis=0, keepdims=True)) * kmax_q)
    q0 = pl.multiple_of(qb * q, q)
    wt = zs_ref[pl.ds(q0, q), :].astype(F32)[:, LANES:].T
    wt = wt[IDX_DIM:IDX_DIM + SUBLANES, :] * ((IDX_HEADS ** -0.5) * (IDX_DIM ** -0.5))

    n_tiles = (q0 + q + tk - 1) // tk
    qpos = lax.broadcasted_iota(jnp.int32, (1, q), 1)
    limit = q0 + ((qpos // CHUNK) + 1) * CHUNK

    def index_tile(t, carry):
        d = _dot(ki_ref[t], qit_ref[...])
        acc = jnp.zeros((tk, q), F32)
        for h in range(IDX_HEADS):
            c0 = _head_col(h, q)
            acc = acc + jnp.maximum(d[:, c0:c0 + q], 0.0) * wt[h:h + 1, :]
        kpos = t * tk + lax.broadcasted_iota(jnp.int32, (tk, 1), 0)
        key_ref[t] = jnp.where(kpos < limit, _sort_key(acc), INT_MIN)
        return carry

    lax.fori_loop(0, n_tiles, index_tile, 0)

    def count_ge(cand):
        def body(t, acc):
            hit = jnp.where(key_ref[t] >= cand, 1.0, 0.0)
            return acc + jnp.sum(hit.reshape(COUNT_CHAINS, tk // (COUNT_CHAINS * SUBLANES), SUBLANES, q), axis=1)
        acc = lax.fori_loop(0, n_tiles, body, jnp.zeros((COUNT_CHAINS, SUBLANES, q), F32))
        return jnp.sum(jnp.sum(acc, axis=0), axis=0, keepdims=True)

    def bit_step(i, thr):
        cand = thr + lax.shift_left(jnp.int32(1), 31 - i)
        return jnp.where(count_ge(cand) >= float(topk), cand, thr)

    thr = lax.fori_loop(0, 32, bit_step, jnp.full((1, q), INT_MIN, jnp.int32))
    need = float(topk) - count_ge(thr + 1)
    need = jnp.where(thr == INT_MIN, 0.0, need)

    def bias_tile(t, run):
        keys = key_ref[t]
        eq = keys == thr
        eqf = jnp.where(eq, 1.0, 0.0)
        rank = _dot(low_ref[...], eqf.astype(BF16)) + run
        tie_ok = jnp.where(eq, rank, float(topk)) < need
        bias = jnp.where(keys > thr, 0.0, jnp.where(tie_ok, 0.0, MASKED))
        return bias, run + jnp.sum(eqf, axis=0, keepdims=True)

    run0 = jnp.zeros((1, q), F32)
    bound = jnp.concatenate([bound[h] for h in sorted(range(B_HEADS), key=lambda h: _head_col(h, q))], axis=1)

    def exact_shift():
        def body(t, carry):
            run, mx = carry
            bias, run = bias_tile(t, run)
            s = _dot(kn_ref[t], qat_ref[...])
            cols = [jnp.max(s[:, c0:c0 + q] + bias, axis=0, keepdims=True) for c0 in range(0, 2 * half, q)]
            return run, jnp.maximum(mx, jnp.concatenate(cols, axis=1))
        return lax.fori_loop(0, n_tiles, body, (run0, jnp.full((1, 2 * half), MASKED, F32)))[1]

    shift_rows = lax.broadcasted_iota(jnp.int32, (2 * SUBLANES, 1), 0) == 0
    qat_ref[B_HD:B_HD + 2 * SUBLANES, :] = jnp.zeros((2 * SUBLANES, 2 * half), BF16)
    shift = lax.cond(jnp.max(bound) > SHIFT_SAFE, exact_shift, lambda: bound)
    qat_ref[B_HD:B_HD + 2 * SUBLANES, :] = jnp.where(shift_rows, -shift, 0.0).astype(BF16)

    acc_ref[...] = jnp.zeros_like(acc_ref)

    def attend(t, run):
        bias, run = bias_tile(t, run)
        s = _dot(kn_ref[t], qat_ref[...])
        for c0 in range(0, 2 * half, q):
            p_ref[:, c0:c0 + q] = jnp.exp(s[:, c0:c0 + q] + bias).astype(BF16)
        acc_ref[:half] += _dot_tn(p_ref[:, :half], va_ref[t])
        acc_ref[half:] += _dot_tn(p_ref[:, half:], vb_ref[t])
        return run

    lax.fori_loop(0, n_tiles, attend, run0)
    for j in range(B_HEADS // 2):
        a = acc_ref[j * q:(j + 1) * q]
        b = acc_ref[half + j * q:half + (j + 1) * q]
        o = jnp.where(lo_half, a / a[:, ONE_COL:ONE_COL + 1], b / b[:, 0:1])
        o_ref[:, j * LANES:(j + 1) * LANES] = o.astype(o_ref.dtype)


def _dsa(z3, q_gain, k_gain, *, tk, qrows):
    bsz, s, _ = z3.shape
    topk = min(TOPK_MAX, s // 4)
    assert tk >= topk >= 2 and s % tk == 0 and tk % LANES == 0 and s % qrows == 0 and qrows % LANES == 0
    n_t = s // tk
    w_q = B_HEADS * B_HD
    low = jnp.asarray(np.tril(np.ones((tk, tk), np.float32), -1), BF16)
    head_of = np.arange(w_q) // B_HD
    bd = jnp.asarray(head_of[:, None] == head_of[None, :], BF16)
    kg = jnp.concatenate([k_gain.astype(F32), jnp.ones((LANES - B_HD,), F32)]).reshape(1, LANES)
    qg = jnp.tile(q_gain.astype(F32) * (B_HD ** -0.5), B_HEADS).reshape(1, w_q)

    def const(a):
        return pl.BlockSpec(a.shape, lambda b, i: (0,) * a.ndim)

    return pl.pallas_call(
        functools.partial(_dsa_kernel, topk, tk),
        grid=(bsz, s // qrows),
        in_specs=[pl.BlockSpec((None, qrows, w_q), lambda b, i: (b, i, Z_BQ // w_q)),
                  pl.BlockSpec((None, qrows, w_q), lambda b, i: (b, i, Z_IQ // w_q)),
                  pl.BlockSpec((None, s, SM_W), lambda b, i: (b, 0, Z_SM // SM_W)),
                  const(qg), const(kg), const(bd), const(low)],
        out_specs=pl.BlockSpec((None, qrows, w_q), lambda b, i: (b, i, 0)),
        out_shape=jax.ShapeDtypeStruct((bsz, s, w_q), BF16),
        scratch_shapes=[pltpu.VMEM((n_t, tk, LANES), BF16),
                        pltpu.VMEM((n_t, tk, LANES), BF16),
                        pltpu.VMEM((n_t, tk, LANES), BF16),
                        pltpu.VMEM((n_t, tk, LANES), BF16),
                        pltpu.VMEM((SUBLANES, LANES), F32),
                        pltpu.VMEM((n_t, tk, qrows), jnp.int32),
                        pltpu.VMEM((LANES, B_HEADS * qrows), BF16),
                        pltpu.VMEM((LANES, IDX_HEADS * qrows), BF16),
                        pltpu.VMEM((tk, B_HEADS * qrows), BF16),
                        pltpu.VMEM((B_HEADS * qrows, LANES), F32)],
        compiler_params=_cparams(("arbitrary", "arbitrary")),
        name="dsa",
    )(z3, z3, z3, qg, kg, bd, low)


def _out_proj_kernel(x_ref, a_ref, b_ref, wa_ref, wb_ref, o_ref):
    o_ref[...] = x_ref[...] + _dot(a_ref[...], wa_ref[...]) + _dot(b_ref[...], wb_ref[...])


def _out_proj(x2d, a2d, b2d, w, *, tm, tn):
    m, d = x2d.shape
    ka = a2d.shape[1]
    kb = b2d.shape[1]
    return pl.pallas_call(
        _out_proj_kernel,
        grid=(m // tm, d // tn),
        in_specs=[pl.BlockSpec((tm, tn), lambda i, j: (i, j)),
                  pl.BlockSpec((tm, ka), lambda i, j: (i, 0)),
                  pl.BlockSpec((tm, kb), lambda i, j: (i, 0)),
                  pl.BlockSpec((ka, tn), lambda i, j: (0, j)),
                  pl.BlockSpec((kb, tn), lambda i, j: (0, j))],
        out_specs=pl.BlockSpec((tm, tn), lambda i, j: (i, j)),
        out_shape=jax.ShapeDtypeStruct((m, d), F32),
        compiler_params=_cparams(("arbitrary", "arbitrary")),
        name="even_out_proj",
    )(x2d, a2d, b2d, w[:ka], w[ka:])


def _causal_conv(u, prev, w_ref, b):
    width = w_ref.shape[0]
    rows = u.shape[0]

    def taps(x):
        y = w_ref[width - 1:width, :] * x + b
        for i in range(width - 1):
            y = y + w_ref[i:i + 1, :] * pltpu.roll(x, width - 1 - i, axis=0)
        return y

    head = taps(jnp.concatenate([prev, u[:SUBLANES]], axis=0))[SUBLANES:]
    return jnp.concatenate([head, taps(u)[SUBLANES:]], axis=0) if rows > SUBLANES else head


def _ffn_kernel(x_ref, g_ref, wg_ref, wv_ref, cwg_ref, cwv_ref, cbg_ref, cbv_ref, wd_ref, o_ref,
                xn_ref, acc_ref, pg_ref, pv_ref):
    s = pl.program_id(1)
    j = pl.program_id(2)
    rows = x_ref.shape[0]

    @pl.when(j == 0)
    def _():
        xn_ref[...] = _rms(x_ref[...], g_ref[...]).astype(BF16)
        acc_ref[...] = jnp.zeros_like(acc_ref)

    @pl.when(s == 0)
    def _():
        pg_ref[j] = jnp.zeros(pg_ref.shape[1:], F32)
        pv_ref[j] = jnp.zeros(pv_ref.shape[1:], F32)

    xn = xn_ref[...]

    def branch(w_ref, cw_ref, cb_ref, prev_ref):
        u = _dot(xn, w_ref[...])
        prev = prev_ref[j]
        prev_ref[j] = u[rows - SUBLANES:]
        return _causal_conv(u, prev, cw_ref, cb_ref[...])

    gate = branch(wg_ref, cwg_ref, cbg_ref, pg_ref)
    val = branch(wv_ref, cwv_ref, cbv_ref, pv_ref)
    act = jax.nn.gelu(gate, approximate=True) * val
    acc_ref[...] += _dot(act.astype(BF16), wd_ref[...])

    @pl.when(j == pl.num_programs(2) - 1)
    def _():
        o_ref[...] = x_ref[...] + acc_ref[...]


def _conv_ffn(h3, gain, w_up, conv_w, conv_b, w_down, *, ts, tf):
    bsz, s, d = h3.shape
    dff = w_down.shape[0]
    nj = dff // tf
    width = conv_w.shape[0]
    cb = conv_b.reshape(1, -1)
    xspec = pl.BlockSpec((None, ts, d), lambda b, i, j: (b, i, 0))
    return pl.pallas_call(
        _ffn_kernel,
        grid=(bsz, s // ts, nj),
        in_specs=[xspec,
                  pl.BlockSpec((1, d), lambda b, i, j: (0, 0)),
                  pl.BlockSpec((d, tf), lambda b, i, j: (0, j)),
                  pl.BlockSpec((d, tf), lambda b, i, j: (0, j + nj)),
                  pl.BlockSpec((width, tf), lambda b, i, j: (0, j)),
                  pl.BlockSpec((width, tf), lambda b, i, j: (0, j + nj)),
                  pl.BlockSpec((1, tf), lambda b, i, j: (0, j)),
                  pl.BlockSpec((1, tf), lambda b, i, j: (0, j + nj)),
                  pl.BlockSpec((tf, d), lambda b, i, j: (j, 0))],
        out_specs=xspec,
        out_shape=jax.ShapeDtypeStruct((bsz, s, d), F32),
        scratch_shapes=[pltpu.VMEM((ts, d), BF16),
                        pltpu.VMEM((ts, d), F32),
                        pltpu.VMEM((nj, SUBLANES, tf), F32),
                        pltpu.VMEM((nj, SUBLANES, tf), F32)],
        compiler_params=_cparams(("arbitrary", "arbitrary", "arbitrary")),
        name="conv_ffn",
    )(h3, gain.reshape(1, d), w_up, w_up, conv_w, conv_w, cb, cb, w_down)


def _rglru_kernel(x_ref, g_ref, win_ref, cw_ref, cb_ref, wg_ref, ba_ref, bx_ref, lam_ref,
                  wout_ref, o_ref, a_ref, u_ref, y_ref, px_ref, ph_ref):
    rows = x_ref.shape[0]
    width = LRU_BLOCKS * LRU_BW

    @pl.when(pl.program_id(1) == 0)
    def _():
        px_ref[...] = jnp.zeros_like(px_ref)
        ph_ref[...] = jnp.zeros_like(ph_ref)

    xn = _rms(x_ref[...], g_ref[...]).astype(BF16)
    lam = lam_ref[...]
    log_sig = jnp.minimum(lam, 0.0) - jnp.log1p(jnp.exp(-jnp.abs(lam)))
    pair = 2 * LRU_BW
    for n2 in range(LRU_BLOCKS // 2):
        cs2 = slice(n2 * pair, (n2 + 1) * pair)
        y_ref[:, cs2] = jax.nn.gelu(_dot(xn, win_ref[:, cs2]), approximate=True)
        xb = _dot(xn, win_ref[:, width + n2 * pair:width + (n2 + 1) * pair])
        prev = px_ref[:, cs2]
        px_ref[:, cs2] = xb[rows - SUBLANES:]
        xc2 = _causal_conv(xb, prev, cw_ref.at[:, cs2], cb_ref[:, cs2])
        for n in (2 * n2, 2 * n2 + 1):
            cs = slice(n * LRU_BW, (n + 1) * LRU_BW)
            xc = xc2[:, (n - 2 * n2) * LRU_BW:(n - 2 * n2 + 1) * LRU_BW]
            rg = _dot(xc.astype(BF16), wg_ref[n])
            r = _sigmoid(rg[:, :LRU_BW] + ba_ref[:, cs])
            gi = _sigmoid(rg[:, LRU_BW:] + bx_ref[:, cs])
            log_a = LRU_C * r * log_sig[:, cs]
            a = jnp.exp(log_a)
            a_ref[:, cs] = a
            u_ref[:, cs] = jnp.sqrt(-jnp.tanh(log_a) * (a * a + 1.0)) * (gi * xc)

    sub = lax.broadcasted_iota(jnp.int32, (SUBLANES, 1), 0)

    def group(gi_, h):
        r0 = pl.multiple_of(gi_ * SUBLANES, SUBLANES)
        a = a_ref[pl.ds(r0, SUBLANES), :]
        b = u_ref[pl.ds(r0, SUBLANES), :]
        d = 1
        while d < SUBLANES:
            keep = sub >= d
            b = jnp.where(keep, a * pltpu.roll(b, d, axis=0) + b, b)
            a = jnp.where(keep, a * pltpu.roll(a, d, axis=0), a)
            d *= 2
        hs = a * h + b
        u_ref[pl.ds(r0, SUBLANES), :] = hs
        return hs[SUBLANES - 1:, :]

    ph_ref[0:1, :] = lax.fori_loop(0, rows // SUBLANES, group, ph_ref[0:1, :], unroll=2)
    o_ref[...] = x_ref[...] + _dot((u_ref[...] * y_ref[...]).astype(BF16), wout_ref[...])


def _rglru(h3, gain, w_in, conv_w, conv_b, wa, ba, wx, bx, lam, w_out, *, ts):
    bsz, s, d = h3.shape
    width = LRU_BLOCKS * LRU_BW
    xspec = pl.BlockSpec((None, ts, d), lambda b, i: (b, i, 0))

    def full(a):
        return pl.BlockSpec(a.shape, lambda b, i: (0,) * a.ndim)

    wg = jnp.concatenate([wa, wx], axis=-1)
    args = (gain.reshape(1, d), w_in, conv_w, conv_b.reshape(1, width), wg, ba.reshape(1, width),
            bx.reshape(1, width), lam.reshape(1, width), w_out)
    return pl.pallas_call(
        _rglru_kernel,
        grid=(bsz, s // ts),
        in_specs=[xspec] + [full(a) for a in args],
        out_specs=xspec,
        out_shape=jax.ShapeDtypeStruct((bsz, s, d), F32),
        scratch_shapes=[pltpu.VMEM((ts, width), F32),
                        pltpu.VMEM((ts, width), F32),
                        pltpu.VMEM((ts, width), F32),
                        pltpu.VMEM((SUBLANES, width), F32),
                        pltpu.VMEM((SUBLANES, width), F32)],
        compiler_params=_cparams(("arbitrary", "arbitrary")),
        name="rglru_block",
    )(h3, *args)


def _tile(n, pref):
    t = min(n, pref)
    assert n % t == 0
    return t


def kernel(x, lb_logits, even_norm, even_w_in, even_w_out, a_out_norm, b_q_norm, b_k_norm, odd_norm, odd_w_in, odd_conv_w, odd_conv_b, odd_gate_a_w, odd_gate_a_b, odd_gate_x_w, odd_gate_x_b, odd_lambda, odd_w_out, ffn_norm, ffn_w_up, ffn_conv_w, ffn_conv_b, ffn_w_down):
    bsz, s, d = x.shape
    depth = ffn_norm.shape[0]
    h = x
    for layer in range(depth):
        j = layer // 2
        if layer % 2 == 0:
            w = even_w_in[j]
            o_bk = 2560
            o_iq = o_bk + 2 * B_HD
            o_ik = o_iq + IDX_HEADS * IDX_DIM
            o_end = o_ik + IDX_DIM + IDX_HEADS
            w = jnp.concatenate(
                [w[:, :o_bk], w[:, o_iq:o_ik], w[:, o_bk:o_iq], w[:, o_ik:o_end],
                 jnp.zeros((d, Z_W - o_end), w.dtype)], axis=1).astype(BF16)
            z = _norm_proj(h.reshape(bsz * s, d), even_norm[j], w,
                           tm=_tile(bsz * s, 512), tn=Z_W, out_dtype=BF16)
            z3 = z.reshape(bsz, s, Z_W)
            a_o = _hgrn(z3, lb_logits, a_out_norm[j], layer, t_rows=_tile(s, 512))
            b_o = _dsa(z3, b_q_norm[j], b_k_norm[j], tk=_tile(s, 512), qrows=_tile(s, 256))
            h = _out_proj(h.reshape(bsz * s, d), a_o.reshape(bsz * s, -1), b_o.reshape(bsz * s, -1),
                          even_w_out[j].astype(BF16), tm=_tile(bsz * s, 512), tn=d).reshape(bsz, s, d)
        else:
            h = _rglru(h, odd_norm[j], odd_w_in[j].astype(BF16), odd_conv_w[j], odd_conv_b[j],
                       odd_gate_a_w[j].astype(BF16), odd_gate_a_b[j], odd_gate_x_w[j].astype(BF16),
                       odd_gate_x_b[j], odd_lambda[j], odd_w_out[j].astype(BF16), ts=_tile(s, 512))
        h = _conv_ffn(h, ffn_norm[layer], ffn_w_up[layer].astype(BF16), ffn_conv_w[layer],
                      ffn_conv_b[layer], ffn_w_down[layer].astype(BF16), ts=_tile(s, 1024), tf=1024)
    return h
```

```python
import functools

import numpy as np
import jax
import jax.numpy as jnp
from jax import lax
from jax.experimental import pallas as pl
from jax.experimental.pallas import tpu as pltpu

F32 = jnp.float32
BF16 = jnp.bfloat16

EPS = 1e-6
CHUNK = 64
A_HEADS = 4
A_D = 128
B_HEADS = 8
B_HD = 64
IDX_HEADS = 8
IDX_DIM = 64
TOPK_MAX = 256
LRU_BLOCKS = 10
LRU_BW = 128
LRU_C = 8.0
LANES = 128
SUBLANES = 8
V7X_VMEM_BYTES = 64 * 1024 * 1024
VMEM_LIMIT = V7X_VMEM_BYTES * 7 // 8

Z_A = 0
Z_BQ = Z_A + 4 * A_HEADS * A_D
Z_IQ = Z_BQ + B_HEADS * B_HD
Z_SM = Z_IQ + IDX_HEADS * IDX_DIM
SM_W = 2 * LANES
Z_W = Z_SM + SM_W
assert 2 * B_HD + IDX_DIM + IDX_HEADS <= SM_W


def _cparams(sem):
    return pltpu.CompilerParams(dimension_semantics=sem, vmem_limit_bytes=VMEM_LIMIT)


def _rms(x, gain):
    ms = jnp.mean(x * x, axis=-1, keepdims=True)
    return x * lax.rsqrt(ms + EPS) * gain


def _sigmoid(x):
    return 1.0 / (1.0 + jnp.exp(-x))


def _dot(a, b):
    return jnp.dot(a, b, preferred_element_type=F32)


def _dot_nt(a, b):
    return lax.dot_general(a, b, (((1,), (1,)), ((), ())), preferred_element_type=F32)


def _lane_tile(x, reps):
    return jnp.concatenate([x] * reps, axis=1)


def _dot_tn(a, b):
    return lax.dot_general(a, b, (((0,), (0,)), ((), ())), preferred_element_type=F32)


def _norm_proj_kernel(x_ref, g_ref, w_ref, o_ref, xn_ref):
    @pl.when(pl.program_id(1) == 0)
    def _():
        xn_ref[...] = _rms(x_ref[...], g_ref[...]).astype(BF16)

    o_ref[...] = _dot(xn_ref[...], w_ref[...]).astype(o_ref.dtype)


def _norm_proj(x2d, gain, w, *, tm, tn, out_dtype):
    m, d = x2d.shape
    n = w.shape[1]
    return pl.pallas_call(
        _norm_proj_kernel,
        grid=(m // tm, n // tn),
        in_specs=[pl.BlockSpec((tm, d), lambda i, j: (i, 0)),
                  pl.BlockSpec((1, d), lambda i, j: (0, 0)),
                  pl.BlockSpec((d, tn), lambda i, j: (0, j))],
        out_specs=pl.BlockSpec((tm, tn), lambda i, j: (i, j)),
        out_shape=jax.ShapeDtypeStruct((m, n), out_dtype),
        scratch_shapes=[pltpu.VMEM((tm, d), BF16)],
        compiler_params=_cparams(("arbitrary", "arbitrary")),
        name="even_in_proj",
    )(x2d, gain.reshape(1, d), w)


def _hgrn_consts(c):
    levels = int(np.log2(c))
    tri = np.tril(np.ones((c, c), np.float32))
    mats = [tri]
    masks = []
    t = np.arange(c)
    for l in range(levels):
        h = c >> (l + 1)
        base = (t // (2 * h)) * (2 * h)
        mid = base + h - 1
        mats.append(tri[mid])
        second = (t % (2 * h)) >= h
        same = base[:, None] == base[None, :]
        masks.append((same & second[:, None] & (~second)[None, :]).astype(np.float32))
    masks.append(np.eye(c, dtype=np.float32))
    return np.concatenate(mats, 0), np.stack(masks, 0), levels


def _hgrn_kernel(layer, c, levels, q_ref, f_ref, i_ref, g_ref, lbl_ref, an_ref, sel_ref, msk_ref,
                 o_ref, st_ref):
    @pl.when(pl.program_id(1) == 0)
    def _():
        st_ref[...] = jnp.zeros_like(st_ref)

    lg = lbl_ref[...]
    e = jnp.exp(lg - jnp.max(lg, axis=0, keepdims=True))
    p = e / jnp.sum(e, axis=0, keepdims=True)
    lb = jnp.sum(p[:layer + 1], axis=0, keepdims=True)
    an = an_ref[...]
    n_chunks = q_ref.shape[0] // c
    width = A_HEADS * A_D

    def chunk(ci, carry):
        r0 = pl.multiple_of(ci * c, c)
        qr = q_ref[pl.ds(r0, c), :].astype(F32)
        fr = f_ref[pl.ds(r0, c), :].astype(F32)
        v = i_ref[pl.ds(r0, c), :].astype(BF16)
        gr = g_ref[pl.ds(r0, c), :].astype(F32)
        q = qr * _sigmoid(qr)
        f = lb + (1.0 - lb) * _sigmoid(fr)
        g = jnp.log(f)
        k = 1.0 - f
        g1 = g.astype(BF16)
        r1 = g - g1.astype(F32)
        g2 = r1.astype(BF16)
        g3 = (r1 - g2.astype(F32)).astype(BF16)
        rs = _dot(sel_ref[...], jnp.concatenate([g1, g2, g3], axis=1))
        rs = rs[:, :width] + rs[:, width:2 * width] + rs[:, 2 * width:]
        b = rs[:c]
        hs = [slice(h * A_D, (h + 1) * A_D) for h in range(A_HEADS)]
        qb_, kb_ = q.astype(BF16), k.astype(BF16)
        scores = [msk_ref[levels] * _dot_nt(qb_[:, s_], kb_[:, s_]) for s_ in hs]
        for l in range(levels):
            x = jnp.exp(-jnp.abs(b - rs[(l + 1) * c:(l + 2) * c]))
            qx, kx = (q * x).astype(BF16), (k * x).astype(BF16)
            scores = [sc + msk_ref[l] * _dot_nt(qx[:, s_], kx[:, s_]) for sc, s_ in zip(scores, hs)]
        qe = (q * jnp.exp(b)).astype(BF16)
        b_last = b[c - 1:c, :]
        kd = (k * jnp.exp(b_last - b)).astype(BF16)
        decay = jnp.exp(b_last)
        gate = gr * _sigmoid(gr)
        for h, s_ in enumerate(hs):
            st = st_ref[h]
            o = _dot_nt(qe[:, s_], st.astype(BF16)) + _dot(scores[h].astype(BF16), v[:, s_])
            st_ref[h] = decay[:, s_] * st + _dot_tn(v[:, s_], kd[:, s_])
            o_ref[pl.ds(r0, c), s_] = (_rms(o, an[:, s_]) * gate[:, s_]).astype(o_ref.dtype)
        return carry

    lax.fori_loop(0, n_chunks, chunk, 0)


def _hgrn(z3, lb_logits, a_norm, layer, *, t_rows):
    bsz, s, _ = z3.shape
    sel, msk, levels = _hgrn_consts(CHUNK)
    sel = jnp.asarray(sel, BF16)
    msk = jnp.asarray(msk, F32)
    nl = lb_logits.shape[0]
    width = A_HEADS * A_D
    col = Z_A // width

    def zspec(k):
        return pl.BlockSpec((None, t_rows, width), lambda b, t: (b, t, col + k))

    return pl.pallas_call(
        functools.partial(_hgrn_kernel, layer, CHUNK, levels),
        grid=(bsz, s // t_rows),
        in_specs=[zspec(0), zspec(1), zspec(2), zspec(3),
                  pl.BlockSpec((nl, width), lambda b, t: (0, 0)),
                  pl.BlockSpec((1, width), lambda b, t: (0, 0)),
                  pl.BlockSpec(sel.shape, lambda b, t: (0, 0)),
                  pl.BlockSpec(msk.shape, lambda b, t: (0, 0, 0))],
        out_specs=pl.BlockSpec((None, t_rows, width), lambda b, t: (b, t, 0)),
        out_shape=jax.ShapeDtypeStruct((bsz, s, width), BF16),
        scratch_shapes=[pltpu.VMEM((A_HEADS, A_D, A_D), F32)],
        compiler_params=_cparams(("arbitrary", "arbitrary")),
        name="hgrn2",
    )(z3, z3, z3, z3, lb_logits, a_norm.reshape(1, -1), sel, msk)


INT_MIN = -2 ** 31
KEY_MASK = 0x7FFFFFFF
HALF_BITS = 16
HALF_MASK = (1 << HALF_BITS) - 1
HALF_MIN = -(1 << (HALF_BITS - 1))
HALF_MAX = (1 << (HALF_BITS - 1)) - 1
MASKED = -1e30
SHIFT_SAFE = 30.0
ONE_COL = B_HD
COUNT_CHAINS = 4

def _sort_key(x):
    bits = pltpu.bitcast(x, jnp.int32)
    key = bits ^ ((bits >> 31) & KEY_MASK)
    return jnp.where(x == 0.0, 0, key)


def _head_col(h, q):
    return ((h % 2) * (B_HEADS // 2) + h // 2) * q


def _dsa_kernel(topk, tk, zq_ref, zi_ref, zs_ref, qg_ref, kg_ref, bd_ref, low_ref, o_ref,
                kn_ref, ki_ref, va_ref, vb_ref, kmax_ref, key_ref, khi_ref, klo_ref, qat_ref, qit_ref, p_ref,
                acc_ref):
    qb = pl.program_id(1)
    q = zq_ref.shape[0]
    half = (B_HEADS // 2) * q
    n_tiles_all = kn_ref.shape[0]
    lane = lax.broadcasted_iota(jnp.int32, (1, LANES), 1)
    lo_half = lane < B_HD
    one_col = jnp.where(lane == ONE_COL, 1.0, 0.0)

    @pl.when(qb == 0)
    def _():
        kg = kg_ref[...]
        qat_ref[...] = jnp.zeros_like(qat_ref)
        qit_ref[...] = jnp.zeros_like(qit_ref)

        def prep(t, kmax):
            r0 = pl.multiple_of(t * tk, tk)
            rows = zs_ref[pl.ds(r0, tk), :].astype(F32)
            kv = rows[:, :LANES]
            ms = jnp.sum(jnp.where(lo_half, kv * kv, 0.0), axis=-1, keepdims=True) * (1.0 / B_HD)
            kn = jnp.where(lo_half, kv * lax.rsqrt(ms + EPS) * kg, 0.0)
            kn_ref[t] = (kn + one_col).astype(BF16)
            ki_ref[t] = jnp.where(lo_half, rows[:, LANES:], 0.0).astype(BF16)
            va_ref[t] = jnp.where(lo_half, pltpu.roll(kv, B_HD, axis=1), one_col).astype(BF16)
            vb_ref[t] = jnp.where(lo_half, jnp.where(lane == 0, 1.0, 0.0), kv).astype(BF16)
            return jnp.maximum(kmax, jnp.max(jnp.sum(kn * kn, axis=-1, keepdims=True), axis=0, keepdims=True))

        kmax = lax.fori_loop(0, n_tiles_all, prep, jnp.zeros((1, 1), F32))
        kmax_ref[...] = jnp.broadcast_to(jnp.sqrt(kmax), kmax_ref.shape)

    qf = zq_ref[...].astype(F32)
    sq = qf * qf
    sq_hi = sq.astype(BF16)
    sq_lo = (sq - sq_hi.astype(F32)).astype(BF16)
    ssum = _dot(sq_hi, bd_ref[...]) + _dot(sq_lo, bd_ref[...])
    qnt = (qf * lax.rsqrt(ssum * (1.0 / B_HD) + EPS) * qg_ref[...]).T
    qit = zi_ref[...].astype(F32).T
    kmax_q = _lane_tile(kmax_ref[0:1, :], q // LANES)
    bound = []
    for h in range(B_HEADS):
        c0 = _head_col(h, q)
        blk = qnt[h * B_HD:(h + 1) * B_HD, :]
        qat_ref[0:B_HD, c0:c0 + q] = blk.astype(BF16)
        qit_ref[0:IDX_DIM, c0:c0 + q] = qit[h * IDX_DIM:(h + 1) * IDX_DIM, :].astype(BF16)
        bound.append(jnp.sqrt(jnp.sum(blk * blk, axis=0, keepdims=True)) * kmax_q)
    q0 = pl.multiple_of(qb * q, q)
    wt = zs_ref[pl.ds(q0, q), :].astype(F32)[:, LANES:].T
    wt = wt[IDX_DIM:IDX_DIM + IDX_HEADS, :] * ((IDX_HEADS ** -0.5) * (IDX_DIM ** -0.5))

    n_tiles = (q0 + q + tk - 1) // tk
    qpos = lax.broadcasted_iota(jnp.int32, (1, q), 1)
    limit = q0 + ((qpos // CHUNK) + 1) * CHUNK

    def index_tile(t, carry):
        d = _dot(ki_ref[t], qit_ref[...])
        acc = jnp.zeros((tk, q), F32)
        for h in range(IDX_HEADS):
            c0 = _head_col(h, q)
            acc = acc + jnp.maximum(d[:, c0:c0 + q], 0.0) * wt[h:h + 1, :]
        kpos = t * tk + lax.broadcasted_iota(jnp.int32, (tk, 1), 0)
        key = jnp.where(kpos < limit, _sort_key(acc), INT_MIN)
        key_ref[t] = key
        khi_ref[t] = (key >> HALF_BITS).astype(jnp.int16)
        klo_ref[t] = ((key & HALF_MASK) + HALF_MIN).astype(jnp.int16)
        return carry

    lax.fori_loop(0, n_tiles, index_tile, 0)

    def count_ge(cand):
        def body(t, acc):
            hit = jnp.where(key_ref[t] >= cand, 1.0, 0.0)
            return acc + jnp.sum(hit.reshape(COUNT_CHAINS, tk // (COUNT_CHAINS * SUBLANES), SUBLANES, q), axis=1)
        acc = lax.fori_loop(0, n_tiles, body, jnp.zeros((COUNT_CHAINS, SUBLANES, q), F32))
        return jnp.sum(jnp.sum(acc, axis=0), axis=0, keepdims=True)

    one16 = jnp.ones((), jnp.int16)
    packed_rows = 2 * SUBLANES

    def count_ge16(cand):
        c16 = cand.astype(jnp.int16)

        def body(t, acc):
            x = jnp.where(khi_ref[t] >= c16, one16, 0 * one16).reshape(tk // packed_rows, packed_rows, q)
            while x.shape[0] > 1:
                x = x[:x.shape[0] // 2] + x[x.shape[0] // 2:]
            return acc + x[0]
        acc = lax.fori_loop(0, n_tiles, body, jnp.zeros((packed_rows, q), jnp.int16))
        return jnp.sum(acc.astype(jnp.int32), axis=0, keepdims=True)

    def half_search():
        def bit_step(i, top):
            cand = top + lax.shift_left(jnp.int32(1), HALF_BITS - 1 - i)
            return jnp.where(count_ge16(cand) >= topk, cand, top)
        return lax.fori_loop(0, HALF_BITS, bit_step, jnp.full((1, q), HALF_MIN, jnp.int32))

    top = half_search()
    top16 = top.astype(jnp.int16)

    def narrow(t, carry):
        hi = khi_ref[t]
        khi_ref[t] = jnp.where(hi > top16, HALF_MAX * one16, jnp.where(hi < top16, HALF_MIN * one16, klo_ref[t]))
        return carry

    lax.fori_loop(0, n_tiles, narrow, 0)
    thr = top * (1 << HALF_BITS) + (half_search() - HALF_MIN)
    need = float(topk) - count_ge(thr + 1)
    need = jnp.where(thr == INT_MIN, 0.0, need)

    def bias_tile(t, run):
        keys = key_ref[t]
        eq = keys == thr
        eqf = jnp.where(eq, 1.0, 0.0)
        rank = _dot(low_ref[...], eqf.astype(BF16)) + run
        tie_ok = jnp.where(eq, rank, float(topk)) < need
        bias = jnp.where(keys > thr, 0.0, jnp.where(tie_ok, 0.0, MASKED))
        return bias, run + jnp.sum(eqf, axis=0, keepdims=True)

    run0 = jnp.zeros((1, q), F32)
    bound = jnp.concatenate([bound[h] for h in sorted(range(B_HEADS), key=lambda h: _head_col(h, q))], axis=1)

    def exact_shift():
        def body(t, carry):
            run, mx = carry
            bias, run = bias_tile(t, run)
            s = _dot(kn_ref[t], qat_ref[...])
            cols = [jnp.max(s[:, c0:c0 + q] + bias, axis=0, keepdims=True) for c0 in range(0, 2 * half, q)]
            return run, jnp.maximum(mx, jnp.concatenate(cols, axis=1))
        return lax.fori_loop(0, n_tiles, body, (run0, jnp.full((1, 2 * half), MASKED, F32)))[1]

    shift_rows = lax.broadcasted_iota(jnp.int32, (2 * SUBLANES, 1), 0) == 0
    qat_ref[B_HD:B_HD + 2 * SUBLANES, :] = jnp.zeros((2 * SUBLANES, 2 * half), BF16)
    shift = lax.cond(jnp.max(bound) > SHIFT_SAFE, exact_shift, lambda: bound)
    qat_ref[B_HD:B_HD + 2 * SUBLANES, :] = jnp.where(shift_rows, -shift, 0.0).astype(BF16)

    acc_ref[...] = jnp.zeros_like(acc_ref)

    def attend(t, run):
        bias, run = bias_tile(t, run)
        s = _dot(kn_ref[t], qat_ref[...])
        for c0 in range(0, 2 * half, q):
            p_ref[:, c0:c0 + q] = jnp.exp(s[:, c0:c0 + q] + bias).astype(BF16)
        acc_ref[:half] += _dot_tn(p_ref[:, :half], va_ref[t])
        acc_ref[half:] += _dot_tn(p_ref[:, half:], vb_ref[t])
        return run

    lax.fori_loop(0, n_tiles, attend, run0)
    for j in range(B_HEADS // 2):
        a = acc_ref[j * q:(j + 1) * q]
        b = acc_ref[half + j * q:half + (j + 1) * q]
        o = jnp.where(lo_half, a / a[:, ONE_COL:ONE_COL + 1], b / b[:, 0:1])
        o_ref[:, j * LANES:(j + 1) * LANES] = o.astype(o_ref.dtype)


def _dsa(z3, q_gain, k_gain, *, tk, qrows):
    bsz, s, _ = z3.shape
    topk = min(TOPK_MAX, s // 4)
    assert tk >= topk >= 2 and s % tk == 0 and tk % LANES == 0 and s % qrows == 0 and qrows % LANES == 0
    groups = tk // (2 * SUBLANES)
    assert groups & (groups - 1) == 0 and s // (2 * SUBLANES) <= HALF_MAX
    n_t = s // tk
    w_q = B_HEADS * B_HD
    low = jnp.asarray(np.tril(np.ones((tk, tk), np.float32), -1), BF16)
    head_of = np.arange(w_q) // B_HD
    bd = jnp.asarray(head_of[:, None] == head_of[None, :], BF16)
    kg = jnp.concatenate([k_gain.astype(F32), jnp.ones((LANES - B_HD,), F32)]).reshape(1, LANES)
    qg = jnp.tile(q_gain.astype(F32) * (B_HD ** -0.5), B_HEADS).reshape(1, w_q)

    def const(a):
        return pl.BlockSpec(a.shape, lambda b, i: (0,) * a.ndim)

    return pl.pallas_call(
        functools.partial(_dsa_kernel, topk, tk),
        grid=(bsz, s // qrows),
        in_specs=[pl.BlockSpec((None, qrows, w_q), lambda b, i: (b, i, Z_BQ // w_q)),
                  pl.BlockSpec((None, qrows, w_q), lambda b, i: (b, i, Z_IQ // w_q)),
                  pl.BlockSpec((None, s, SM_W), lambda b, i: (b, 0, Z_SM // SM_W)),
                  const(qg), const(kg), const(bd), const(low)],
        out_specs=pl.BlockSpec((None, qrows, w_q), lambda b, i: (b, i, 0)),
        out_shape=jax.ShapeDtypeStruct((bsz, s, w_q), BF16),
        scratch_shapes=[pltpu.VMEM((n_t, tk, LANES), BF16),
                        pltpu.VMEM((n_t, tk, LANES), BF16),
                        pltpu.VMEM((n_t, tk, LANES), BF16),
                        pltpu.VMEM((n_t, tk, LANES), BF16),
                        pltpu.VMEM((SUBLANES, LANES), F32),
                        pltpu.VMEM((n_t, tk, qrows), jnp.int32),
                        pltpu.VMEM((n_t, tk, qrows), jnp.int16),
                        pltpu.VMEM((n_t, tk, qrows), jnp.int16),
                        pltpu.VMEM((LANES, B_HEADS * qrows), BF16),
                        pltpu.VMEM((LANES, IDX_HEADS * qrows), BF16),
                        pltpu.VMEM((tk, B_HEADS * qrows), BF16),
                        pltpu.VMEM((B_HEADS * qrows, LANES), F32)],
        compiler_params=_cparams(("arbitrary", "arbitrary")),
        name="dsa",
    )(z3, z3, z3, qg, kg, bd, low)


def _out_proj_kernel(x_ref, a_ref, b_ref, wa_ref, wb_ref, o_ref):
    o_ref[...] = x_ref[...] + _dot(a_ref[...], wa_ref[...]) + _dot(b_ref[...], wb_ref[...])


def _out_proj(x2d, a2d, b2d, w, *, tm, tn):
    m, d = x2d.shape
    ka = a2d.shape[1]
    kb = b2d.shape[1]
    return pl.pallas_call(
        _out_proj_kernel,
        grid=(m // tm, d // tn),
        in_specs=[pl.BlockSpec((tm, tn), lambda i, j: (i, j)),
                  pl.BlockSpec((tm, ka), lambda i, j: (i, 0)),
                  pl.BlockSpec((tm, kb), lambda i, j: (i, 0)),
                  pl.BlockSpec((ka, tn), lambda i, j: (0, j)),
                  pl.BlockSpec((kb, tn), lambda i, j: (0, j))],
        out_specs=pl.BlockSpec((tm, tn), lambda i, j: (i, j)),
        out_shape=jax.ShapeDtypeStruct((m, d), F32),
        compiler_params=_cparams(("arbitrary", "arbitrary")),
        name="even_out_proj",
    )(x2d, a2d, b2d, w[:ka], w[ka:])


def _causal_conv(u, prev, w_ref, b):
    width = w_ref.shape[0]
    rows = u.shape[0]

    def taps(x):
        y = w_ref[width - 1:width, :] * x + b
        for i in range(width - 1):
            y = y + w_ref[i:i + 1, :] * pltpu.roll(x, width - 1 - i, axis=0)
        return y

    head = taps(jnp.concatenate([prev, u[:SUBLANES]], axis=0))[SUBLANES:]
    return jnp.concatenate([head, taps(u)[SUBLANES:]], axis=0) if rows > SUBLANES else head


def _ffn_kernel(x_ref, g_ref, wg_ref, wv_ref, cwg_ref, cwv_ref, cbg_ref, cbv_ref, wd_ref, o_ref,
                xn_ref, acc_ref, pg_ref, pv_ref):
    s = pl.program_id(1)
    j = pl.program_id(2)
    rows = x_ref.shape[0]

    @pl.when(j == 0)
    def _():
        xn_ref[...] = _rms(x_ref[...], g_ref[...]).astype(BF16)
        acc_ref[...] = jnp.zeros_like(acc_ref)

    @pl.when(s == 0)
    def _():
        pg_ref[j] = jnp.zeros(pg_ref.shape[1:], F32)
        pv_ref[j] = jnp.zeros(pv_ref.shape[1:], F32)

    xn = xn_ref[...]

    def branch(w_ref, cw_ref, cb_ref, prev_ref):
        u = _dot(xn, w_ref[...])
        prev = prev_ref[j]
        prev_ref[j] = u[rows - SUBLANES:]
        return _causal_conv(u, prev, cw_ref, cb_ref[...])

    gate = branch(wg_ref, cwg_ref, cbg_ref, pg_ref)
    val = branch(wv_ref, cwv_ref, cbv_ref, pv_ref)
    act = jax.nn.gelu(gate, approximate=True) * val
    acc_ref[...] += _dot(act.astype(BF16), wd_ref[...])

    @pl.when(j == pl.num_programs(2) - 1)
    def _():
        o_ref[...] = x_ref[...] + acc_ref[...]


def _conv_ffn(h3, gain, w_up, conv_w, conv_b, w_down, *, ts, tf):
    bsz, s, d = h3.shape
    dff = w_down.shape[0]
    nj = dff // tf
    width = conv_w.shape[0]
    cb = conv_b.reshape(1, -1)
    xspec = pl.BlockSpec((None, ts, d), lambda b, i, j: (b, i, 0))
    return pl.pallas_call(
        _ffn_kernel,
        grid=(bsz, s // ts, nj),
        in_specs=[xspec,
                  pl.BlockSpec((1, d), lambda b, i, j: (0, 0)),
                  pl.BlockSpec((d, tf), lambda b, i, j: (0, j)),
                  pl.BlockSpec((d, tf), lambda b, i, j: (0, j + nj)),
                  pl.BlockSpec((width, tf), lambda b, i, j: (0, j)),
                  pl.BlockSpec((width, tf), lambda b, i, j: (0, j + nj)),
                  pl.BlockSpec((1, tf), lambda b, i, j: (0, j)),
                  pl.BlockSpec((1, tf), lambda b, i, j: (0, j + nj)),
                  pl.BlockSpec((tf, d), lambda b, i, j: (j, 0))],
        out_specs=xspec,
        out_shape=jax.ShapeDtypeStruct((bsz, s, d), F32),
        scratch_shapes=[pltpu.VMEM((ts, d), BF16),
                        pltpu.VMEM((ts, d), F32),
                        pltpu.VMEM((nj, SUBLANES, tf), F32),
                        pltpu.VMEM((nj, SUBLANES, tf), F32)],
        compiler_params=_cparams(("arbitrary", "arbitrary", "arbitrary")),
        name="conv_ffn",
    )(h3, gain.reshape(1, d), w_up, w_up, conv_w, conv_w, cb, cb, w_down)


def _rglru_kernel(x_ref, g_ref, win_ref, cw_ref, cb_ref, wg_ref, ba_ref, bx_ref, lam_ref,
                  wout_ref, o_ref, a_ref, u_ref, y_ref, px_ref, ph_ref):
    rows = x_ref.shape[0]
    width = LRU_BLOCKS * LRU_BW

    @pl.when(pl.program_id(1) == 0)
    def _():
        px_ref[...] = jnp.zeros_like(px_ref)
        ph_ref[...] = jnp.zeros_like(ph_ref)

    xn = _rms(x_ref[...], g_ref[...]).astype(BF16)
    lam = lam_ref[...]
    log_sig = jnp.minimum(lam, 0.0) - jnp.log1p(jnp.exp(-jnp.abs(lam)))
    pair = 2 * LRU_BW
    for n2 in range(LRU_BLOCKS // 2):
        cs2 = slice(n2 * pair, (n2 + 1) * pair)
        y_ref[:, cs2] = jax.nn.gelu(_dot(xn, win_ref[:, cs2]), approximate=True)
        xb = _dot(xn, win_ref[:, width + n2 * pair:width + (n2 + 1) * pair])
        prev = px_ref[:, cs2]
        px_ref[:, cs2] = xb[rows - SUBLANES:]
        xc2 = _causal_conv(xb, prev, cw_ref.at[:, cs2], cb_ref[:, cs2])
        for n in (2 * n2, 2 * n2 + 1):
            cs = slice(n * LRU_BW, (n + 1) * LRU_BW)
            xc = xc2[:, (n - 2 * n2) * LRU_BW:(n - 2 * n2 + 1) * LRU_BW]
            rg = _dot(xc.astype(BF16), wg_ref[n])
            r = _sigmoid(rg[:, :LRU_BW] + ba_ref[:, cs])
            gi = _sigmoid(rg[:, LRU_BW:] + bx_ref[:, cs])
            log_a = LRU_C * r * log_sig[:, cs]
            a = jnp.exp(log_a)
            a_ref[:, cs] = a
            u_ref[:, cs] = jnp.sqrt(-jnp.tanh(log_a) * (a * a + 1.0)) * (gi * xc)

    sub = lax.broadcasted_iota(jnp.int32, (SUBLANES, 1), 0)

    def group(gi_, h):
        r0 = pl.multiple_of(gi_ * SUBLANES, SUBLANES)
        a = a_ref[pl.ds(r0, SUBLANES), :]
        b = u_ref[pl.ds(r0, SUBLANES), :]
        d = 1
        while d < SUBLANES:
            keep = sub >= d
            b = jnp.where(keep, a * pltpu.roll(b, d, axis=0) + b, b)
            a = jnp.where(keep, a * pltpu.roll(a, d, axis=0), a)
            d *= 2
        hs = a * h + b
        u_ref[pl.ds(r0, SUBLANES), :] = hs
        return hs[SUBLANES - 1:, :]

    ph_ref[0:1, :] = lax.fori_loop(0, rows // SUBLANES, group, ph_ref[0:1, :], unroll=2)
    o_ref[...] = x_ref[...] + _dot((u_ref[...] * y_ref[...]).astype(BF16), wout_ref[...])


def _rglru(h3, gain, w_in, conv_w, conv_b, wa, ba, wx, bx, lam, w_out, *, ts):
    bsz, s, d = h3.shape
    width = LRU_BLOCKS * LRU_BW
    xspec = pl.BlockSpec((None, ts, d), lambda b, i: (b, i, 0))

    def full(a):
        return pl.BlockSpec(a.shape, lambda b, i: (0,) * a.ndim)

    wg = jnp.concatenate([wa, wx], axis=-1)
    args = (gain.reshape(1, d), w_in, conv_w, conv_b.reshape(1, width), wg, ba.reshape(1, width),
            bx.reshape(1, width), lam.reshape(1, width), w_out)
    return pl.pallas_call(
        _rglru_kernel,
        grid=(bsz, s // ts),
        in_specs=[xspec] + [full(a) for a in args],
        out_specs=xspec,
        out_shape=jax.ShapeDtypeStruct((bsz, s, d), F32),
        scratch_shapes=[pltpu.VMEM((ts, width), F32),
                        pltpu.VMEM((ts, width), F32),
                        pltpu.VMEM((ts, width), F32),
                        pltpu.VMEM((SUBLANES, width), F32),
                        pltpu.VMEM((SUBLANES, width), F32)],
        compiler_params=_cparams(("arbitrary", "arbitrary")),
        name="rglru_block",
    )(h3, *args)


def _tile(n, pref):
    t = min(n, pref)
    assert n % t == 0
    return t


def _tiles(bsz, s):
    return dict(
        proj_rows=_tile(bsz * s, 512),
        hgrn_rows=_tile(s, 512),
        dsa_keys=_tile(s, 512),
        dsa_queries=_tile(s, 256),
        lru_rows=_tile(s, 512),
        ffn_rows=_tile(s, 1024),
        ffn_cols=1024,
    )


def kernel(x, lb_logits, even_norm, even_w_in, even_w_out, a_out_norm, b_q_norm, b_k_norm, odd_norm, odd_w_in, odd_conv_w, odd_conv_b, odd_gate_a_w, odd_gate_a_b, odd_gate_x_w, odd_gate_x_b, odd_lambda, odd_w_out, ffn_norm, ffn_w_up, ffn_conv_w, ffn_conv_b, ffn_w_down):
    bsz, s, d = x.shape
    depth = ffn_norm.shape[0]
    t = _tiles(bsz, s)
    h = x
    for layer in range(depth):
        j = layer // 2
        if layer % 2 == 0:
            w = even_w_in[j]
            o_bk = Z_IQ
            o_iq = o_bk + 2 * B_HD
            o_ik = o_iq + IDX_HEADS * IDX_DIM
            o_end = o_ik + IDX_DIM + IDX_HEADS
            assert w.shape[1] == o_end
            w = jnp.concatenate(
                [w[:, :o_bk], w[:, o_iq:o_ik], w[:, o_bk:o_iq], w[:, o_ik:o_end],
                 jnp.zeros((d, Z_W - o_end), w.dtype)], axis=1).astype(BF16)
            z = _norm_proj(h.reshape(bsz * s, d), even_norm[j], w,
                           tm=t["proj_rows"], tn=Z_W, out_dtype=BF16)
            z3 = z.reshape(bsz, s, Z_W)
            a_o = _hgrn(z3, lb_logits, a_out_norm[j], layer, t_rows=t["hgrn_rows"])
            b_o = _dsa(z3, b_q_norm[j], b_k_norm[j], tk=t["dsa_keys"], qrows=t["dsa_queries"])
            h = _out_proj(h.reshape(bsz * s, d), a_o.reshape(bsz * s, -1), b_o.reshape(bsz * s, -1),
                          even_w_out[j].astype(BF16), tm=t["proj_rows"], tn=d).reshape(bsz, s, d)
        else:
            h = _rglru(h, odd_norm[j], odd_w_in[j].astype(BF16), odd_conv_w[j], odd_conv_b[j],
                       odd_gate_a_w[j].astype(BF16), odd_gate_a_b[j], odd_gate_x_w[j].astype(BF16),
                       odd_gate_x_b[j], odd_lambda[j], odd_w_out[j].astype(BF16), ts=t["lru_rows"])
        h = _conv_ffn(h, ffn_norm[layer], ffn_w_up[layer].astype(BF16), ffn_conv_w[layer],
                      ffn_conv_b[layer], ffn_w_down[layer].astype(BF16), ts=t["ffn_rows"], tf=t["ffn_cols"])
    return h
```

```python
import functools

import numpy as np
import jax
import jax.numpy as jnp
from jax import lax
from jax.experimental import pallas as pl
from jax.experimental.pallas import tpu as pltpu

F32 = jnp.float32
BF16 = jnp.bfloat16

EPS = 1e-6
CHUNK = 64
A_HEADS = 4
A_D = 128
B_HEADS = 8
B_HD = 64
IDX_HEADS = 8
IDX_DIM = 64
TOPK_MAX = 256
LRU_BLOCKS = 10
LRU_BW = 128
LRU_C = 8.0
LANES = 128
SUBLANES = 8
V7X_VMEM_BYTES = 64 * 1024 * 1024
VMEM_LIMIT = V7X_VMEM_BYTES * 7 // 8

Z_A = 0
Z_BQ = Z_A + 4 * A_HEADS * A_D
Z_IQ = Z_BQ + B_HEADS * B_HD
Z_SM = Z_IQ + IDX_HEADS * IDX_DIM
SM_W = 2 * LANES
Z_W = Z_SM + SM_W
assert 2 * B_HD + IDX_DIM + IDX_HEADS <= SM_W


def _cparams(sem):
    return pltpu.CompilerParams(dimension_semantics=sem, vmem_limit_bytes=VMEM_LIMIT)


def _rms(x, gain):
    ms = jnp.mean(x * x, axis=-1, keepdims=True)
    return x * lax.rsqrt(ms + EPS) * gain


def _sigmoid(x):
    return 1.0 / (1.0 + jnp.exp(-x))


def _dot(a, b):
    return jnp.dot(a, b, preferred_element_type=F32)


def _dot_nt(a, b):
    return lax.dot_general(a, b, (((1,), (1,)), ((), ())), preferred_element_type=F32)


def _lane_tile(x, reps):
    return jnp.concatenate([x] * reps, axis=1)


def _dot_tn(a, b):
    return lax.dot_general(a, b, (((0,), (0,)), ((), ())), preferred_element_type=F32)


def _norm_proj_kernel(x_ref, g_ref, w_ref, o_ref, xn_ref):
    @pl.when(pl.program_id(1) == 0)
    def _():
        xn_ref[...] = _rms(x_ref[...], g_ref[...]).astype(BF16)

    o_ref[...] = _dot(xn_ref[...], w_ref[...]).astype(o_ref.dtype)


def _norm_proj(x2d, gain, w, *, tm, tn, out_dtype):
    m, d = x2d.shape
    n = w.shape[1]
    return pl.pallas_call(
        _norm_proj_kernel,
        grid=(m // tm, n // tn),
        in_specs=[pl.BlockSpec((tm, d), lambda i, j: (i, 0)),
                  pl.BlockSpec((1, d), lambda i, j: (0, 0)),
                  pl.BlockSpec((d, tn), lambda i, j: (0, j))],
        out_specs=pl.BlockSpec((tm, tn), lambda i, j: (i, j)),
        out_shape=jax.ShapeDtypeStruct((m, n), out_dtype),
        scratch_shapes=[pltpu.VMEM((tm, d), BF16)],
        compiler_params=_cparams(("arbitrary", "arbitrary")),
        name="even_in_proj",
    )(x2d, gain.reshape(1, d), w)


def _hgrn_consts(c):
    levels = int(np.log2(c))
    tri = np.tril(np.ones((c, c), np.float32))
    masks = []
    t = np.arange(c)
    for l in range(levels):
        h = c >> (l + 1)
        base = (t // (2 * h)) * (2 * h)
        second = (t % (2 * h)) >= h
        same = base[:, None] == base[None, :]
        masks.append((same & second[:, None] & (~second)[None, :]).astype(np.float32))
    masks.append(np.eye(c, dtype=np.float32))
    return tri, np.stack(masks, 0), levels


def _hgrn_kernel(layer, c, levels, q_ref, f_ref, i_ref, g_ref, lbl_ref, an_ref, tri_ref, msk_ref,
                 o_ref, st_ref, b_ref):
    @pl.when(pl.program_id(1) == 0)
    def _():
        st_ref[...] = jnp.zeros_like(st_ref)

    lg = lbl_ref[...]
    e = jnp.exp(lg - jnp.max(lg, axis=0, keepdims=True))
    p = e / jnp.sum(e, axis=0, keepdims=True)
    lb = jnp.sum(p[:layer + 1], axis=0, keepdims=True)
    an = an_ref[...]
    n_chunks = q_ref.shape[1] // c
    width = A_HEADS * A_D
    row_pos = lax.broadcasted_iota(jnp.int32, (c, 1), 0)

    def chunk_row(bb, r0):
        qr = q_ref[bb, pl.ds(r0, c), :].astype(F32)
        fr = f_ref[bb, pl.ds(r0, c), :].astype(F32)
        v = i_ref[bb, pl.ds(r0, c), :].astype(BF16)
        gr = g_ref[bb, pl.ds(r0, c), :].astype(F32)
        q = qr * _sigmoid(qr)
        f = lb + (1.0 - lb) * _sigmoid(fr)
        g = jnp.log(f)
        k = 1.0 - f
        g1 = g.astype(BF16)
        r1 = g - g1.astype(F32)
        g2 = r1.astype(BF16)
        g3 = (r1 - g2.astype(F32)).astype(BF16)
        rs = _dot(tri_ref[...], jnp.concatenate([g1, g2, g3], axis=1))
        b = rs[:, :width] + rs[:, width:2 * width] + rs[:, 2 * width:]
        b_ref[bb] = b

        def midpoint(l):
            h = c >> (l + 1)
            if 2 * h >= SUBLANES:
                return jnp.concatenate([jnp.broadcast_to(b_ref[bb, base + h - 1:base + h, :], (2 * h, width))
                                        for base in range(0, c, 2 * h)], axis=0)
            out = b
            for k_ in range(2 * h):
                if k_ != h - 1:
                    out = jnp.where(row_pos % (2 * h) == k_, pltpu.roll(b, (k_ - h + 1) % c, axis=0), out)
            return out

        hs = [slice(h * A_D, (h + 1) * A_D) for h in range(A_HEADS)]
        qb_, kb_ = q.astype(BF16), k.astype(BF16)
        scores = [msk_ref[levels] * _dot_nt(qb_[:, s_], kb_[:, s_]) for s_ in hs]
        for l in range(levels):
            x = jnp.exp(-jnp.abs(b - midpoint(l)))
            qx, kx = (q * x).astype(BF16), (k * x).astype(BF16)
            scores = [sc + msk_ref[l] * _dot_nt(qx[:, s_], kx[:, s_]) for sc, s_ in zip(scores, hs)]
        qe = (q * jnp.exp(b)).astype(BF16)
        b_last = b[c - 1:c, :]
        kd = (k * jnp.exp(b_last - b)).astype(BF16)
        decay = jnp.exp(b_last)
        gate = gr * _sigmoid(gr)
        for h, s_ in enumerate(hs):
            st = st_ref[bb, h]
            o = _dot_nt(qe[:, s_], st.astype(BF16)) + _dot(scores[h].astype(BF16), v[:, s_])
            st_ref[bb, h] = decay[:, s_] * st + _dot_tn(v[:, s_], kd[:, s_])
            o_ref[bb, pl.ds(r0, c), s_] = (_rms(o, an[:, s_]) * gate[:, s_]).astype(o_ref.dtype)

    def chunk(ci, carry):
        r0 = pl.multiple_of(ci * c, c)
        for bb in range(q_ref.shape[0]):
            chunk_row(bb, r0)
        return carry

    lax.fori_loop(0, n_chunks, chunk, 0)


def _hgrn(z3, lb_logits, a_norm, layer, *, t_rows, b_rows):
    bsz, s, _ = z3.shape
    tri, msk, levels = _hgrn_consts(CHUNK)
    tri = jnp.asarray(tri, BF16)
    msk = jnp.asarray(msk, F32)
    nl = lb_logits.shape[0]
    width = A_HEADS * A_D
    col = Z_A // width

    def zspec(k):
        return pl.BlockSpec((b_rows, t_rows, width), lambda b, t: (b, t, col + k))

    return pl.pallas_call(
        functools.partial(_hgrn_kernel, layer, CHUNK, levels),
        grid=(bsz // b_rows, s // t_rows),
        in_specs=[zspec(0), zspec(1), zspec(2), zspec(3),
                  pl.BlockSpec((nl, width), lambda b, t: (0, 0)),
                  pl.BlockSpec((1, width), lambda b, t: (0, 0)),
                  pl.BlockSpec(tri.shape, lambda b, t: (0, 0)),
                  pl.BlockSpec(msk.shape, lambda b, t: (0, 0, 0))],
        out_specs=pl.BlockSpec((b_rows, t_rows, width), lambda b, t: (b, t, 0)),
        out_shape=jax.ShapeDtypeStruct((bsz, s, width), BF16),
        scratch_shapes=[pltpu.VMEM((b_rows, A_HEADS, A_D, A_D), F32),
                        pltpu.VMEM((b_rows, CHUNK, width), F32)],
        compiler_params=_cparams(("arbitrary", "arbitrary")),
        name="hgrn2",
    )(z3, z3, z3, z3, lb_logits, a_norm.reshape(1, -1), tri, msk)


INT_MIN = -2 ** 31
KEY_MASK = 0x7FFFFFFF
HALF_BITS = 16
HALF_MASK = (1 << HALF_BITS) - 1
HALF_MIN = -(1 << (HALF_BITS - 1))
HALF_MAX = (1 << (HALF_BITS - 1)) - 1
MASKED = -1e30
SHIFT_SAFE = 30.0
ONE_COL = B_HD
COUNT_CHAINS = 4

def _sort_key(x):
    bits = pltpu.bitcast(x, jnp.int32)
    key = bits ^ ((bits >> 31) & KEY_MASK)
    return jnp.where(x == 0.0, 0, key)


def _head_col(h, q):
    return ((h % 2) * (B_HEADS // 2) + h // 2) * q


def _dsa_kernel(topk, tk, zq_ref, zi_ref, zs_ref, qg_ref, kg_ref, bd_ref, low_ref, o_ref,
                kn_ref, ki_ref, va_ref, vb_ref, kmax_ref, key_ref, khi_ref, klo_ref, qat_ref, qit_ref, p_ref,
                acc_ref):
    qb = pl.program_id(1)
    q = zq_ref.shape[0]
    half = (B_HEADS // 2) * q
    n_tiles_all = kn_ref.shape[0]
    lane = lax.broadcasted_iota(jnp.int32, (1, LANES), 1)
    lo_half = lane < B_HD
    one_col = jnp.where(lane == ONE_COL, 1.0, 0.0)

    @pl.when(qb == 0)
    def _():
        kg = kg_ref[...]
        qat_ref[...] = jnp.zeros_like(qat_ref)
        qit_ref[...] = jnp.zeros_like(qit_ref)

        def prep(t, kmax):
            r0 = pl.multiple_of(t * tk, tk)
            rows = zs_ref[pl.ds(r0, tk), :].astype(F32)
            kv = rows[:, :LANES]
            ms = jnp.sum(jnp.where(lo_half, kv * kv, 0.0), axis=-1, keepdims=True) * (1.0 / B_HD)
            kn = jnp.where(lo_half, kv * lax.rsqrt(ms + EPS) * kg, 0.0)
            kn_ref[t] = (kn + one_col).astype(BF16)
            ki_ref[t] = jnp.where(lo_half, rows[:, LANES:], 0.0).astype(BF16)
            va_ref[t] = jnp.where(lo_half, pltpu.roll(kv, B_HD, axis=1), one_col).astype(BF16)
            vb_ref[t] = jnp.where(lo_half, jnp.where(lane == 0, 1.0, 0.0), kv).astype(BF16)
            return jnp.maximum(kmax, jnp.max(jnp.sum(kn * kn, axis=-1, keepdims=True), axis=0, keepdims=True))

        kmax = lax.fori_loop(0, n_tiles_all, prep, jnp.zeros((1, 1), F32))
        kmax_ref[...] = jnp.broadcast_to(jnp.sqrt(kmax), kmax_ref.shape)

    qf = zq_ref[...].astype(F32)
    sq = qf * qf
    sq_hi = sq.astype(BF16)
    sq_lo = (sq - sq_hi.astype(F32)).astype(BF16)
    ssum = _dot(sq_hi, bd_ref[...]) + _dot(sq_lo, bd_ref[...])
    qnt = (qf * lax.rsqrt(ssum * (1.0 / B_HD) + EPS) * qg_ref[...]).T
    qit = zi_ref[...].astype(F32).T
    kmax_q = _lane_tile(kmax_ref[0:1, :], q // LANES)
    bound = []
    for h in range(B_HEADS):
        c0 = _head_col(h, q)
        blk = qnt[h * B_HD:(h + 1) * B_HD, :]
        qat_ref[0:B_HD, c0:c0 + q] = blk.astype(BF16)
        qit_ref[0:IDX_DIM, c0:c0 + q] = qit[h * IDX_DIM:(h + 1) * IDX_DIM, :].astype(BF16)
        bound.append(jnp.sqrt(jnp.sum(blk * blk, axis=0, keepdims=True)) * kmax_q)
    q0 = pl.multiple_of(qb * q, q)
    wt = zs_ref[pl.ds(q0, q), :].astype(F32)[:, LANES:].T
    wt = wt[IDX_DIM:IDX_DIM + IDX_HEADS, :] * ((IDX_HEADS ** -0.5) * (IDX_DIM ** -0.5))

    n_tiles = (q0 + q + tk - 1) // tk
    qpos = lax.broadcasted_iota(jnp.int32, (1, q), 1)
    limit = q0 + ((qpos // CHUNK) + 1) * CHUNK

    def index_tile(t, carry):
        d = _dot(ki_ref[t], qit_ref[...])
        acc = jnp.zeros((tk, q), F32)
        for h in range(IDX_HEADS):
            c0 = _head_col(h, q)
            acc = acc + jnp.maximum(d[:, c0:c0 + q], 0.0) * wt[h:h + 1, :]
        kpos = t * tk + lax.broadcasted_iota(jnp.int32, (tk, 1), 0)
        key = jnp.where(kpos < limit, _sort_key(acc), INT_MIN)
        key_ref[t] = key
        khi_ref[t] = (key >> HALF_BITS).astype(jnp.int16)
        klo_ref[t] = ((key & HALF_MASK) + HALF_MIN).astype(jnp.int16)
        return carry

    lax.fori_loop(0, n_tiles, index_tile, 0)

    def count_ge(cand):
        def body(t, acc):
            hit = jnp.where(key_ref[t] >= cand, 1.0, 0.0)
            return acc + jnp.sum(hit.reshape(COUNT_CHAINS, tk // (COUNT_CHAINS * SUBLANES), SUBLANES, q), axis=1)
        acc = lax.fori_loop(0, n_tiles, body, jnp.zeros((COUNT_CHAINS, SUBLANES, q), F32))
        return jnp.sum(jnp.sum(acc, axis=0), axis=0, keepdims=True)

    one16 = jnp.ones((), jnp.int16)
    packed_rows = 2 * SUBLANES

    def count_ge16(cand):
        c16 = cand.astype(jnp.int16)

        def body(t, acc):
            x = jnp.where(khi_ref[t] >= c16, one16, 0 * one16).reshape(tk // packed_rows, packed_rows, q)
            while x.shape[0] > 1:
                x = x[:x.shape[0] // 2] + x[x.shape[0] // 2:]
            return acc + x[0]
        acc = lax.fori_loop(0, n_tiles, body, jnp.zeros((packed_rows, q), jnp.int16))
        return jnp.sum(acc.astype(jnp.int32), axis=0, keepdims=True)

    def half_search():
        def bit_step(i, top):
            cand = top + lax.shift_left(jnp.int32(1), HALF_BITS - 1 - i)
            return jnp.where(count_ge16(cand) >= topk, cand, top)
        return lax.fori_loop(0, HALF_BITS, bit_step, jnp.full((1, q), HALF_MIN, jnp.int32))

    top = half_search()
    top16 = top.astype(jnp.int16)

    def narrow(t, carry):
        hi = khi_ref[t]
        khi_ref[t] = jnp.where(hi > top16, HALF_MAX * one16, jnp.where(hi < top16, HALF_MIN * one16, klo_ref[t]))
        return carry

    lax.fori_loop(0, n_tiles, narrow, 0)
    thr = top * (1 << HALF_BITS) + (half_search() - HALF_MIN)
    need = float(topk) - count_ge(thr + 1)
    need = jnp.where(thr == INT_MIN, 0.0, need)

    def bias_tile(t, run):
        keys = key_ref[t]
        eq = keys == thr
        eqf = jnp.where(eq, 1.0, 0.0)
        rank = _dot(low_ref[...], eqf.astype(BF16)) + run
        tie_ok = jnp.where(eq, rank, float(topk)) < need
        bias = jnp.where(keys > thr, 0.0, jnp.where(tie_ok, 0.0, MASKED))
        return bias, run + jnp.sum(eqf, axis=0, keepdims=True)

    run0 = jnp.zeros((1, q), F32)
    bound = jnp.concatenate([bound[h] for h in sorted(range(B_HEADS), key=lambda h: _head_col(h, q))], axis=1)

    def exact_shift():
        def body(t, carry):
            run, mx = carry
            bias, run = bias_tile(t, run)
            s = _dot(kn_ref[t], qat_ref[...])
            cols = [jnp.max(s[:, c0:c0 + q] + bias, axis=0, keepdims=True) for c0 in range(0, 2 * half, q)]
            return run, jnp.maximum(mx, jnp.concatenate(cols, axis=1))
        return lax.fori_loop(0, n_tiles, body, (run0, jnp.full((1, 2 * half), MASKED, F32)))[1]

    shift_rows = lax.broadcasted_iota(jnp.int32, (2 * SUBLANES, 1), 0) == 0
    qat_ref[B_HD:B_HD + 2 * SUBLANES, :] = jnp.zeros((2 * SUBLANES, 2 * half), BF16)
    shift = lax.cond(jnp.max(bound) > SHIFT_SAFE, exact_shift, lambda: bound)
    qat_ref[B_HD:B_HD + 2 * SUBLANES, :] = jnp.where(shift_rows, -shift, 0.0).astype(BF16)

    acc_ref[...] = jnp.zeros_like(acc_ref)

    def attend(t, run):
        bias, run = bias_tile(t, run)
        s = _dot(kn_ref[t], qat_ref[...])
        for c0 in range(0, 2 * half, q):
            p_ref[:, c0:c0 + q] = jnp.exp(s[:, c0:c0 + q] + bias).astype(BF16)
        acc_ref[:half] += _dot_tn(p_ref[:, :half], va_ref[t])
        acc_ref[half:] += _dot_tn(p_ref[:, half:], vb_ref[t])
        return run

    lax.fori_loop(0, n_tiles, attend, run0)
    for j in range(B_HEADS // 2):
        a = acc_ref[j * q:(j + 1) * q]
        b = acc_ref[half + j * q:half + (j + 1) * q]
        o = jnp.where(lo_half, a / a[:, ONE_COL:ONE_COL + 1], b / b[:, 0:1])
        o_ref[:, j * LANES:(j + 1) * LANES] = o.astype(o_ref.dtype)


def _dsa(z3, q_gain, k_gain, *, tk, qrows):
    bsz, s, _ = z3.shape
    topk = min(TOPK_MAX, s // 4)
    assert tk >= topk >= 2 and s % tk == 0 and tk % LANES == 0 and s % qrows == 0 and qrows % LANES == 0
    groups = tk // (2 * SUBLANES)
    assert groups & (groups - 1) == 0 and s // (2 * SUBLANES) <= HALF_MAX
    n_t = s // tk
    w_q = B_HEADS * B_HD
    low = jnp.asarray(np.tril(np.ones((tk, tk), np.float32), -1), BF16)
    head_of = np.arange(w_q) // B_HD
    bd = jnp.asarray(head_of[:, None] == head_of[None, :], BF16)
    kg = jnp.concatenate([k_gain.astype(F32), jnp.ones((LANES - B_HD,), F32)]).reshape(1, LANES)
    qg = jnp.tile(q_gain.astype(F32) * (B_HD ** -0.5), B_HEADS).reshape(1, w_q)

    def const(a):
        return pl.BlockSpec(a.shape, lambda b, i: (0,) * a.ndim)

    return pl.pallas_call(
        functools.partial(_dsa_kernel, topk, tk),
        grid=(bsz, s // qrows),
        in_specs=[pl.BlockSpec((None, qrows, w_q), lambda b, i: (b, i, Z_BQ // w_q)),
                  pl.BlockSpec((None, qrows, w_q), lambda b, i: (b, i, Z_IQ // w_q)),
                  pl.BlockSpec((None, s, SM_W), lambda b, i: (b, 0, Z_SM // SM_W)),
                  const(qg), const(kg), const(bd), const(low)],
        out_specs=pl.BlockSpec((None, qrows, w_q), lambda b, i: (b, i, 0)),
        out_shape=jax.ShapeDtypeStruct((bsz, s, w_q), BF16),
        scratch_shapes=[pltpu.VMEM((n_t, tk, LANES), BF16),
                        pltpu.VMEM((n_t, tk, LANES), BF16),
                        pltpu.VMEM((n_t, tk, LANES), BF16),
                        pltpu.VMEM((n_t, tk, LANES), BF16),
                        pltpu.VMEM((SUBLANES, LANES), F32),
                        pltpu.VMEM((n_t, tk, qrows), jnp.int32),
                        pltpu.VMEM((n_t, tk, qrows), jnp.int16),
                        pltpu.VMEM((n_t, tk, qrows), jnp.int16),
                        pltpu.VMEM((LANES, B_HEADS * qrows), BF16),
                        pltpu.VMEM((LANES, IDX_HEADS * qrows), BF16),
                        pltpu.VMEM((tk, B_HEADS * qrows), BF16),
                        pltpu.VMEM((B_HEADS * qrows, LANES), F32)],
        compiler_params=_cparams(("arbitrary", "arbitrary")),
        name="dsa",
    )(z3, z3, z3, qg, kg, bd, low)


def _out_proj_kernel(x_ref, a_ref, b_ref, wa_ref, wb_ref, o_ref):
    o_ref[...] = x_ref[...] + _dot(a_ref[...], wa_ref[...]) + _dot(b_ref[...], wb_ref[...])


def _out_proj(x2d, a2d, b2d, w, *, tm, tn):
    m, d = x2d.shape
    ka = a2d.shape[1]
    kb = b2d.shape[1]
    return pl.pallas_call(
        _out_proj_kernel,
        grid=(m // tm, d // tn),
        in_specs=[pl.BlockSpec((tm, tn), lambda i, j: (i, j)),
                  pl.BlockSpec((tm, ka), lambda i, j: (i, 0)),
                  pl.BlockSpec((tm, kb), lambda i, j: (i, 0)),
                  pl.BlockSpec((ka, tn), lambda i, j: (0, j)),
                  pl.BlockSpec((kb, tn), lambda i, j: (0, j))],
        out_specs=pl.BlockSpec((tm, tn), lambda i, j: (i, j)),
        out_shape=jax.ShapeDtypeStruct((m, d), F32),
        compiler_params=_cparams(("arbitrary", "arbitrary")),
        name="even_out_proj",
    )(x2d, a2d, b2d, w[:ka], w[ka:])


def _causal_conv(u, prev, w_ref, b):
    width = w_ref.shape[0]
    rows = u.shape[0]

    def taps(x):
        y = w_ref[width - 1:width, :] * x + b
        for i in range(width - 1):
            y = y + w_ref[i:i + 1, :] * pltpu.roll(x, width - 1 - i, axis=0)
        return y

    head = taps(jnp.concatenate([prev, u[:SUBLANES]], axis=0))[SUBLANES:]
    return jnp.concatenate([head, taps(u)[SUBLANES:]], axis=0) if rows > SUBLANES else head


def _ffn_kernel(x_ref, g_ref, wg_ref, wv_ref, cwg_ref, cwv_ref, cbg_ref, cbv_ref, wd_ref, o_ref,
                xn_ref, acc_ref, pg_ref, pv_ref):
    s = pl.program_id(1)
    j = pl.program_id(2)
    rows = x_ref.shape[0]

    @pl.when(j == 0)
    def _():
        xn_ref[...] = _rms(x_ref[...], g_ref[...]).astype(BF16)
        acc_ref[...] = jnp.zeros_like(acc_ref)

    @pl.when(s == 0)
    def _():
        pg_ref[j] = jnp.zeros(pg_ref.shape[1:], F32)
        pv_ref[j] = jnp.zeros(pv_ref.shape[1:], F32)

    xn = xn_ref[...]

    def branch(w_ref, cw_ref, cb_ref, prev_ref):
        u = _dot(xn, w_ref[...])
        prev = prev_ref[j]
        prev_ref[j] = u[rows - SUBLANES:]
        return _causal_conv(u, prev, cw_ref, cb_ref[...])

    gate = branch(wg_ref, cwg_ref, cbg_ref, pg_ref)
    val = branch(wv_ref, cwv_ref, cbv_ref, pv_ref)
    act = jax.nn.gelu(gate, approximate=True) * val
    acc_ref[...] += _dot(act.astype(BF16), wd_ref[...])

    @pl.when(j == pl.num_programs(2) - 1)
    def _():
        o_ref[...] = x_ref[...] + acc_ref[...]


def _conv_ffn(h3, gain, w_up, conv_w, conv_b, w_down, *, ts, tf):
    bsz, s, d = h3.shape
    dff = w_down.shape[0]
    nj = dff // tf
    width = conv_w.shape[0]
    cb = conv_b.reshape(1, -1)
    xspec = pl.BlockSpec((None, ts, d), lambda b, i, j: (b, i, 0))
    return pl.pallas_call(
        _ffn_kernel,
        grid=(bsz, s // ts, nj),
        in_specs=[xspec,
                  pl.BlockSpec((1, d), lambda b, i, j: (0, 0)),
                  pl.BlockSpec((d, tf), lambda b, i, j: (0, j)),
                  pl.BlockSpec((d, tf), lambda b, i, j: (0, j + nj)),
                  pl.BlockSpec((width, tf), lambda b, i, j: (0, j)),
                  pl.BlockSpec((width, tf), lambda b, i, j: (0, j + nj)),
                  pl.BlockSpec((1, tf), lambda b, i, j: (0, j)),
                  pl.BlockSpec((1, tf), lambda b, i, j: (0, j + nj)),
                  pl.BlockSpec((tf, d), lambda b, i, j: (j, 0))],
        out_specs=xspec,
        out_shape=jax.ShapeDtypeStruct((bsz, s, d), F32),
        scratch_shapes=[pltpu.VMEM((ts, d), BF16),
                        pltpu.VMEM((ts, d), F32),
                        pltpu.VMEM((nj, SUBLANES, tf), F32),
                        pltpu.VMEM((nj, SUBLANES, tf), F32)],
        compiler_params=_cparams(("arbitrary", "arbitrary", "arbitrary")),
        name="conv_ffn",
    )(h3, gain.reshape(1, d), w_up, w_up, conv_w, conv_w, cb, cb, w_down)


def _rglru_kernel(x_ref, g_ref, win_ref, cw_ref, cb_ref, wg_ref, ba_ref, bx_ref, lam_ref,
                  wout_ref, o_ref, a_ref, u_ref, y_ref, px_ref, ph_ref):
    rows = x_ref.shape[0]
    width = LRU_BLOCKS * LRU_BW

    @pl.when(pl.program_id(1) == 0)
    def _():
        px_ref[...] = jnp.zeros_like(px_ref)
        ph_ref[...] = jnp.zeros_like(ph_ref)

    xn = _rms(x_ref[...], g_ref[...]).astype(BF16)
    lam = lam_ref[...]
    log_sig = jnp.minimum(lam, 0.0) - jnp.log1p(jnp.exp(-jnp.abs(lam)))
    pair = 2 * LRU_BW
    for n2 in range(LRU_BLOCKS // 2):
        cs2 = slice(n2 * pair, (n2 + 1) * pair)
        y_ref[:, cs2] = jax.nn.gelu(_dot(xn, win_ref[:, cs2]), approximate=True)
        xb = _dot(xn, win_ref[:, width + n2 * pair:width + (n2 + 1) * pair])
        prev = px_ref[:, cs2]
        px_ref[:, cs2] = xb[rows - SUBLANES:]
        xc2 = _causal_conv(xb, prev, cw_ref.at[:, cs2], cb_ref[:, cs2])
        for n in (2 * n2, 2 * n2 + 1):
            cs = slice(n * LRU_BW, (n + 1) * LRU_BW)
            xc = xc2[:, (n - 2 * n2) * LRU_BW:(n - 2 * n2 + 1) * LRU_BW]
            rg = _dot(xc.astype(BF16), wg_ref[n])
            r = _sigmoid(rg[:, :LRU_BW] + ba_ref[:, cs])
            gi = _sigmoid(rg[:, LRU_BW:] + bx_ref[:, cs])
            log_a = LRU_C * r * log_sig[:, cs]
            a = jnp.exp(log_a)
            a_ref[:, cs] = a
            u_ref[:, cs] = jnp.sqrt(-jnp.tanh(log_a) * (a * a + 1.0)) * (gi * xc)

    sub = lax.broadcasted_iota(jnp.int32, (SUBLANES, 1), 0)

    def group(gi_, h):
        r0 = pl.multiple_of(gi_ * SUBLANES, SUBLANES)
        a = a_ref[pl.ds(r0, SUBLANES), :]
        b = u_ref[pl.ds(r0, SUBLANES), :]
        d = 1
        while d < SUBLANES:
            keep = sub >= d
            b = jnp.where(keep, a * pltpu.roll(b, d, axis=0) + b, b)
            a = jnp.where(keep, a * pltpu.roll(a, d, axis=0), a)
            d *= 2
        hs = a * h + b
        u_ref[pl.ds(r0, SUBLANES), :] = hs
        return hs[SUBLANES - 1:, :]

    ph_ref[0:1, :] = lax.fori_loop(0, rows // SUBLANES, group, ph_ref[0:1, :], unroll=2)
    o_ref[...] = x_ref[...] + _dot((u_ref[...] * y_ref[...]).astype(BF16), wout_ref[...])


def _rglru(h3, gain, w_in, conv_w, conv_b, wa, ba, wx, bx, lam, w_out, *, ts):
    bsz, s, d = h3.shape
    width = LRU_BLOCKS * LRU_BW
    xspec = pl.BlockSpec((None, ts, d), lambda b, i: (b, i, 0))

    def full(a):
        return pl.BlockSpec(a.shape, lambda b, i: (0,) * a.ndim)

    wg = jnp.concatenate([wa, wx], axis=-1)
    args = (gain.reshape(1, d), w_in, conv_w, conv_b.reshape(1, width), wg, ba.reshape(1, width),
            bx.reshape(1, width), lam.reshape(1, width), w_out)
    return pl.pallas_call(
        _rglru_kernel,
        grid=(bsz, s // ts),
        in_specs=[xspec] + [full(a) for a in args],
        out_specs=xspec,
        out_shape=jax.ShapeDtypeStruct((bsz, s, d), F32),
        scratch_shapes=[pltpu.VMEM((ts, width), F32),
                        pltpu.VMEM((ts, width), F32),
                        pltpu.VMEM((ts, width), F32),
                        pltpu.VMEM((SUBLANES, width), F32),
                        pltpu.VMEM((SUBLANES, width), F32)],
        compiler_params=_cparams(("arbitrary", "arbitrary")),
        name="rglru_block",
    )(h3, *args)


def _tile(n, pref):
    t = min(n, pref)
    assert n % t == 0
    return t


def _tiles(bsz, s):
    return dict(
        proj_rows=_tile(bsz * s, 512),
        hgrn_rows=_tile(s, 512),
        hgrn_batch=_tile(bsz, 2),
        dsa_keys=_tile(s, 512),
        dsa_queries=_tile(s, 256),
        lru_rows=_tile(s, 512),
        ffn_rows=_tile(s, 1024),
        ffn_cols=1024,
    )


def kernel(x, lb_logits, even_norm, even_w_in, even_w_out, a_out_norm, b_q_norm, b_k_norm, odd_norm, odd_w_in, odd_conv_w, odd_conv_b, odd_gate_a_w, odd_gate_a_b, odd_gate_x_w, odd_gate_x_b, odd_lambda, odd_w_out, ffn_norm, ffn_w_up, ffn_conv_w, ffn_conv_b, ffn_w_down):
    bsz, s, d = x.shape
    depth = ffn_norm.shape[0]
    t = _tiles(bsz, s)
    h = x
    for layer in range(depth):
        j = layer // 2
        if layer % 2 == 0:
            w = even_w_in[j]
            o_bk = Z_IQ
            o_iq = o_bk + 2 * B_HD
            o_ik = o_iq + IDX_HEADS * IDX_DIM
            o_end = o_ik + IDX_DIM + IDX_HEADS
            assert w.shape[1] == o_end
            w = jnp.concatenate(
                [w[:, :o_bk], w[:, o_iq:o_ik], w[:, o_bk:o_iq], w[:, o_ik:o_end],
                 jnp.zeros((d, Z_W - o_end), w.dtype)], axis=1).astype(BF16)
            z = _norm_proj(h.reshape(bsz * s, d), even_norm[j], w,
                           tm=t["proj_rows"], tn=Z_W, out_dtype=BF16)
            z3 = z.reshape(bsz, s, Z_W)
            a_o = _hgrn(z3, lb_logits, a_out_norm[j], layer, t_rows=t["hgrn_rows"], b_rows=t["hgrn_batch"])
            b_o = _dsa(z3, b_q_norm[j], b_k_norm[j], tk=t["dsa_keys"], qrows=t["dsa_queries"])
            h = _out_proj(h.reshape(bsz * s, d), a_o.reshape(bsz * s, -1), b_o.reshape(bsz * s, -1),
                          even_w_out[j].astype(BF16), tm=t["proj_rows"], tn=d).reshape(bsz, s, d)
        else:
            h = _rglru(h, odd_norm[j], odd_w_in[j].astype(BF16), odd_conv_w[j], odd_conv_b[j],
                       odd_gate_a_w[j].astype(BF16), odd_gate_a_b[j], odd_gate_x_w[j].astype(BF16),
                       odd_gate_x_b[j], odd_lambda[j], odd_w_out[j].astype(BF16), ts=t["lru_rows"])
        h = _conv_ffn(h, ffn_norm[layer], ffn_w_up[layer].astype(BF16), ffn_conv_w[layer],
                      ffn_conv_b[layer], ffn_w_down[layer].astype(BF16), ts=t["ffn_rows"], tf=t["ffn_cols"])
    return h
```

```python
import functools

import numpy as np
import jax
import jax.numpy as jnp
from jax import lax
from jax.experimental import pallas as pl
from jax.experimental.pallas import tpu as pltpu

F32 = jnp.float32
BF16 = jnp.bfloat16

EPS = 1e-6
CHUNK = 64
A_HEADS = 4
A_D = 128
B_HEADS = 8
B_HD = 64
IDX_HEADS = 8
IDX_DIM = 64
TOPK_MAX = 256
LRU_BLOCKS = 10
LRU_BW = 128
LRU_C = 8.0
LANES = 128
SUBLANES = 8
V7X_VMEM_BYTES = 64 * 1024 * 1024
VMEM_LIMIT = V7X_VMEM_BYTES * 7 // 8

Z_A = 0
Z_BQ = Z_A + 4 * A_HEADS * A_D
Z_IQ = Z_BQ + B_HEADS * B_HD
Z_SM = Z_IQ + IDX_HEADS * IDX_DIM
SM_W = 2 * LANES
Z_W = Z_SM + SM_W
assert 2 * B_HD + IDX_DIM + IDX_HEADS <= SM_W


def _cparams(sem):
    return pltpu.CompilerParams(dimension_semantics=sem, vmem_limit_bytes=VMEM_LIMIT)


def _rms(x, gain):
    ms = jnp.mean(x * x, axis=-1, keepdims=True)
    return x * lax.rsqrt(ms + EPS) * gain


def _sigmoid(x):
    return 1.0 / (1.0 + jnp.exp(-x))


GELU_C = float(np.sqrt(2.0 / np.pi))
GELU_C3 = GELU_C * 0.044715


def _gelu_tanh(x):
    half = 0.5 * x
    return half + half * jnp.tanh(x * (GELU_C + GELU_C3 * (x * x)))


def _dot(a, b):
    return jnp.dot(a, b, preferred_element_type=F32)


def _dot_nt(a, b):
    return lax.dot_general(a, b, (((1,), (1,)), ((), ())), preferred_element_type=F32)


def _lane_tile(x, reps):
    return jnp.concatenate([x] * reps, axis=1)


def _dot_tn(a, b):
    return lax.dot_general(a, b, (((0,), (0,)), ((), ())), preferred_element_type=F32)


def _norm_proj_kernel(x_ref, g_ref, w_ref, o_ref, xn_ref):
    @pl.when(pl.program_id(1) == 0)
    def _():
        xn_ref[...] = _rms(x_ref[...], g_ref[...]).astype(BF16)

    o_ref[...] = _dot(xn_ref[...], w_ref[...]).astype(o_ref.dtype)


def _norm_proj(x2d, gain, w, *, tm, tn, out_dtype):
    m, d = x2d.shape
    n = w.shape[1]
    return pl.pallas_call(
        _norm_proj_kernel,
        grid=(m // tm, n // tn),
        in_specs=[pl.BlockSpec((tm, d), lambda i, j: (i, 0)),
                  pl.BlockSpec((1, d), lambda i, j: (0, 0)),
                  pl.BlockSpec((d, tn), lambda i, j: (0, j))],
        out_specs=pl.BlockSpec((tm, tn), lambda i, j: (i, j)),
        out_shape=jax.ShapeDtypeStruct((m, n), out_dtype),
        scratch_shapes=[pltpu.VMEM((tm, d), BF16)],
        compiler_params=_cparams(("arbitrary", "arbitrary")),
        name="even_in_proj",
    )(x2d, gain.reshape(1, d), w)


def _hgrn_consts(c):
    levels = int(np.log2(c))
    tri = np.tril(np.ones((c, c), np.float32))
    masks = []
    t = np.arange(c)
    for l in range(levels):
        h = c >> (l + 1)
        base = (t // (2 * h)) * (2 * h)
        second = (t % (2 * h)) >= h
        same = base[:, None] == base[None, :]
        masks.append((same & second[:, None] & (~second)[None, :]).astype(np.float32))
    masks.append(np.eye(c, dtype=np.float32))
    return tri, np.stack(masks, 0), levels


def _hgrn_kernel(layer, c, levels, q_ref, f_ref, i_ref, g_ref, lbl_ref, an_ref, tri_ref, msk_ref,
                 o_ref, st_ref, b_ref):
    @pl.when(pl.program_id(1) == 0)
    def _():
        st_ref[...] = jnp.zeros_like(st_ref)

    lg = lbl_ref[...]
    e = jnp.exp(lg - jnp.max(lg, axis=0, keepdims=True))
    p = e / jnp.sum(e, axis=0, keepdims=True)
    lb = jnp.sum(p[:layer + 1], axis=0, keepdims=True)
    an = an_ref[...]
    n_chunks = q_ref.shape[1] // c
    width = A_HEADS * A_D
    row_pos = lax.broadcasted_iota(jnp.int32, (c, 1), 0)

    def chunk_row(bb, r0):
        qr = q_ref[bb, pl.ds(r0, c), :].astype(F32)
        fr = f_ref[bb, pl.ds(r0, c), :].astype(F32)
        v = i_ref[bb, pl.ds(r0, c), :].astype(BF16)
        gr = g_ref[bb, pl.ds(r0, c), :].astype(F32)
        q = qr * _sigmoid(qr)
        f = lb + (1.0 - lb) * _sigmoid(fr)
        g = jnp.log(f)
        k = 1.0 - f
        g1 = g.astype(BF16)
        r1 = g - g1.astype(F32)
        g2 = r1.astype(BF16)
        g3 = (r1 - g2.astype(F32)).astype(BF16)
        rs = _dot(tri_ref[...], jnp.concatenate([g1, g2, g3], axis=1))
        b = rs[:, :width] + rs[:, width:2 * width] + rs[:, 2 * width:]
        b_ref[bb] = b

        def midpoint(l):
            h = c >> (l + 1)
            if 2 * h >= SUBLANES:
                return jnp.concatenate([jnp.broadcast_to(b_ref[bb, base + h - 1:base + h, :], (2 * h, width))
                                        for base in range(0, c, 2 * h)], axis=0)
            out = b
            for k_ in range(2 * h):
                if k_ != h - 1:
                    out = jnp.where(row_pos % (2 * h) == k_, pltpu.roll(b, (k_ - h + 1) % c, axis=0), out)
            return out

        hs = [slice(h * A_D, (h + 1) * A_D) for h in range(A_HEADS)]
        qb_, kb_ = q.astype(BF16), k.astype(BF16)
        scores = [msk_ref[levels] * _dot_nt(qb_[:, s_], kb_[:, s_]) for s_ in hs]
        for l in range(levels):
            x = jnp.exp(-jnp.abs(b - midpoint(l)))
            qx, kx = (q * x).astype(BF16), (k * x).astype(BF16)
            scores = [sc + msk_ref[l] * _dot_nt(qx[:, s_], kx[:, s_]) for sc, s_ in zip(scores, hs)]
        qe = (q * jnp.exp(b)).astype(BF16)
        b_last = b[c - 1:c, :]
        kd = (k * jnp.exp(b_last - b)).astype(BF16)
        decay = jnp.exp(b_last)
        gate = gr * _sigmoid(gr)
        for h, s_ in enumerate(hs):
            st = st_ref[bb, h]
            o = _dot_nt(qe[:, s_], st.astype(BF16)) + _dot(scores[h].astype(BF16), v[:, s_])
            st_ref[bb, h] = decay[:, s_] * st + _dot_tn(v[:, s_], kd[:, s_])
            o_ref[bb, pl.ds(r0, c), s_] = (_rms(o, an[:, s_]) * gate[:, s_]).astype(o_ref.dtype)

    def chunk(ci, carry):
        r0 = pl.multiple_of(ci * c, c)
        for bb in range(q_ref.shape[0]):
            chunk_row(bb, r0)
        return carry

    lax.fori_loop(0, n_chunks, chunk, 0)


def _hgrn(z3, lb_logits, a_norm, layer, *, t_rows, b_rows):
    bsz, s, _ = z3.shape
    tri, msk, levels = _hgrn_consts(CHUNK)
    tri = jnp.asarray(tri, BF16)
    msk = jnp.asarray(msk, F32)
    nl = lb_logits.shape[0]
    width = A_HEADS * A_D
    col = Z_A // width

    def zspec(k):
        return pl.BlockSpec((b_rows, t_rows, width), lambda b, t: (b, t, col + k))

    return pl.pallas_call(
        functools.partial(_hgrn_kernel, layer, CHUNK, levels),
        grid=(bsz // b_rows, s // t_rows),
        in_specs=[zspec(0), zspec(1), zspec(2), zspec(3),
                  pl.BlockSpec((nl, width), lambda b, t: (0, 0)),
                  pl.BlockSpec((1, width), lambda b, t: (0, 0)),
                  pl.BlockSpec(tri.shape, lambda b, t: (0, 0)),
                  pl.BlockSpec(msk.shape, lambda b, t: (0, 0, 0))],
        out_specs=pl.BlockSpec((b_rows, t_rows, width), lambda b, t: (b, t, 0)),
        out_shape=jax.ShapeDtypeStruct((bsz, s, width), BF16),
        scratch_shapes=[pltpu.VMEM((b_rows, A_HEADS, A_D, A_D), F32),
                        pltpu.VMEM((b_rows, CHUNK, width), F32)],
        compiler_params=_cparams(("arbitrary", "arbitrary")),
        name="hgrn2",
    )(z3, z3, z3, z3, lb_logits, a_norm.reshape(1, -1), tri, msk)


INT_MIN = -2 ** 31
KEY_MASK = 0x7FFFFFFF
HALF_BITS = 16
HALF_MASK = (1 << HALF_BITS) - 1
HALF_MIN = -(1 << (HALF_BITS - 1))
HALF_MAX = (1 << (HALF_BITS - 1)) - 1
MASKED = -1e30
SHIFT_SAFE = 30.0
ONE_COL = B_HD
COUNT_CHAINS = 4

def _sort_key(x):
    bits = pltpu.bitcast(x, jnp.int32)
    key = bits ^ ((bits >> 31) & KEY_MASK)
    return jnp.where(x == 0.0, 0, key)


def _head_col(h, q):
    return ((h % 2) * (B_HEADS // 2) + h // 2) * q


def _dsa_kernel(topk, tk, zq_ref, zi_ref, zs_ref, qg_ref, kg_ref, bd_ref, low_ref, o_ref,
                kn_ref, ki_ref, va_ref, vb_ref, kmax_ref, key_ref, khi_ref, klo_ref, qat_ref, qit_ref, p_ref,
                acc_ref):
    qb = pl.program_id(1)
    q = zq_ref.shape[0]
    half = (B_HEADS // 2) * q
    n_tiles_all = kn_ref.shape[0]
    lane = lax.broadcasted_iota(jnp.int32, (1, LANES), 1)
    lo_half = lane < B_HD
    one_col = jnp.where(lane == ONE_COL, 1.0, 0.0)

    @pl.when(qb == 0)
    def _():
        kg = kg_ref[...]
        qat_ref[...] = jnp.zeros_like(qat_ref)
        qit_ref[...] = jnp.zeros_like(qit_ref)

        def prep(t, kmax):
            r0 = pl.multiple_of(t * tk, tk)
            rows = zs_ref[pl.ds(r0, tk), :].astype(F32)
            kv = rows[:, :LANES]
            ms = jnp.sum(jnp.where(lo_half, kv * kv, 0.0), axis=-1, keepdims=True) * (1.0 / B_HD)
            kn = jnp.where(lo_half, kv * lax.rsqrt(ms + EPS) * kg, 0.0)
            kn_ref[t] = (kn + one_col).astype(BF16)
            ki_ref[t] = jnp.where(lo_half, rows[:, LANES:], 0.0).astype(BF16)
            va_ref[t] = jnp.where(lo_half, pltpu.roll(kv, B_HD, axis=1), one_col).astype(BF16)
            vb_ref[t] = jnp.where(lo_half, jnp.where(lane == 0, 1.0, 0.0), kv).astype(BF16)
            return jnp.maximum(kmax, jnp.max(jnp.sum(kn * kn, axis=-1, keepdims=True), axis=0, keepdims=True))

        kmax = lax.fori_loop(0, n_tiles_all, prep, jnp.zeros((1, 1), F32))
        kmax_ref[...] = jnp.broadcast_to(jnp.sqrt(kmax), kmax_ref.shape)

    qf = zq_ref[...].astype(F32)
    sq = qf * qf
    sq_hi = sq.astype(BF16)
    sq_lo = (sq - sq_hi.astype(F32)).astype(BF16)
    ssum = _dot(sq_hi, bd_ref[...]) + _dot(sq_lo, bd_ref[...])
    qnt = (qf * lax.rsqrt(ssum * (1.0 / B_HD) + EPS) * qg_ref[...]).T
    qit = zi_ref[...].astype(F32).T
    kmax_q = _lane_tile(kmax_ref[0:1, :], q // LANES)
    bound = []
    for h in range(B_HEADS):
        c0 = _head_col(h, q)
        blk = qnt[h * B_HD:(h + 1) * B_HD, :]
        qat_ref[0:B_HD, c0:c0 + q] = blk.astype(BF16)
        qit_ref[0:IDX_DIM, c0:c0 + q] = qit[h * IDX_DIM:(h + 1) * IDX_DIM, :].astype(BF16)
        bound.append(jnp.sqrt(jnp.sum(blk * blk, axis=0, keepdims=True)) * kmax_q)
    q0 = pl.multiple_of(qb * q, q)
    wt = zs_ref[pl.ds(q0, q), :].astype(F32)[:, LANES:].T
    wt = wt[IDX_DIM:IDX_DIM + IDX_HEADS, :] * ((IDX_HEADS ** -0.5) * (IDX_DIM ** -0.5))

    n_tiles = (q0 + q + tk - 1) // tk
    qpos = lax.broadcasted_iota(jnp.int32, (1, q), 1)
    limit = q0 + ((qpos // CHUNK) + 1) * CHUNK

    def index_tile(t, carry):
        d = _dot(ki_ref[t], qit_ref[...])
        acc = jnp.zeros((tk, q), F32)
        for h in range(IDX_HEADS):
            c0 = _head_col(h, q)
            acc = acc + jnp.maximum(d[:, c0:c0 + q], 0.0) * wt[h:h + 1, :]
        kpos = t * tk + lax.broadcasted_iota(jnp.int32, (tk, 1), 0)
        key = jnp.where(kpos < limit, _sort_key(acc), INT_MIN)
        key_ref[t] = key
        khi_ref[t] = (key >> HALF_BITS).astype(jnp.int16)
        klo_ref[t] = ((key & HALF_MASK) + HALF_MIN).astype(jnp.int16)
        return carry

    lax.fori_loop(0, n_tiles, index_tile, 0)

    def count_ge(cand):
        def body(t, acc):
            hit = jnp.where(key_ref[t] >= cand, 1.0, 0.0)
            return acc + jnp.sum(hit.reshape(COUNT_CHAINS, tk // (COUNT_CHAINS * SUBLANES), SUBLANES, q), axis=1)
        acc = lax.fori_loop(0, n_tiles, body, jnp.zeros((COUNT_CHAINS, SUBLANES, q), F32))
        return jnp.sum(jnp.sum(acc, axis=0), axis=0, keepdims=True)

    one16 = jnp.ones((), jnp.int16)
    packed_rows = 2 * SUBLANES

    def count_ge16(cand):
        c16 = cand.astype(jnp.int16)

        def body(t, acc):
            x = jnp.where(khi_ref[t] >= c16, one16, 0 * one16).reshape(tk // packed_rows, packed_rows, q)
            while x.shape[0] > 1:
                x = x[:x.shape[0] // 2] + x[x.shape[0] // 2:]
            return acc + x[0]
        acc = lax.fori_loop(0, n_tiles, body, jnp.zeros((packed_rows, q), jnp.int16))
        return jnp.sum(acc.astype(jnp.int32), axis=0, keepdims=True)

    def half_search():
        def bit_step(i, top):
            cand = top + lax.shift_left(jnp.int32(1), HALF_BITS - 1 - i)
            return jnp.where(count_ge16(cand) >= topk, cand, top)
        return lax.fori_loop(0, HALF_BITS, bit_step, jnp.full((1, q), HALF_MIN, jnp.int32))

    top = half_search()
    top16 = top.astype(jnp.int16)

    def narrow(t, carry):
        hi = khi_ref[t]
        khi_ref[t] = jnp.where(hi > top16, HALF_MAX * one16, jnp.where(hi < top16, HALF_MIN * one16, klo_ref[t]))
        return carry

    lax.fori_loop(0, n_tiles, narrow, 0)
    thr = top * (1 << HALF_BITS) + (half_search() - HALF_MIN)
    need = float(topk) - count_ge(thr + 1)
    need = jnp.where(thr == INT_MIN, 0.0, need)

    def bias_tile(t, run):
        keys = key_ref[t]
        eq = keys == thr
        eqf = jnp.where(eq, 1.0, 0.0)
        rank = _dot(low_ref[...], eqf.astype(BF16)) + run
        tie_ok = jnp.where(eq, rank, float(topk)) < need
        bias = jnp.where(keys > thr, 0.0, jnp.where(tie_ok, 0.0, MASKED))
        return bias, run + jnp.sum(eqf, axis=0, keepdims=True)

    run0 = jnp.zeros((1, q), F32)
    bound = jnp.concatenate([bound[h] for h in sorted(range(B_HEADS), key=lambda h: _head_col(h, q))], axis=1)

    def exact_shift():
        def body(t, carry):
            run, mx = carry
            bias, run = bias_tile(t, run)
            s = _dot(kn_ref[t], qat_ref[...])
            cols = [jnp.max(s[:, c0:c0 + q] + bias, axis=0, keepdims=True) for c0 in range(0, 2 * half, q)]
            return run, jnp.maximum(mx, jnp.concatenate(cols, axis=1))
        return lax.fori_loop(0, n_tiles, body, (run0, jnp.full((1, 2 * half), MASKED, F32)))[1]

    shift_rows = lax.broadcasted_iota(jnp.int32, (2 * SUBLANES, 1), 0) == 0
    qat_ref[B_HD:B_HD + 2 * SUBLANES, :] = jnp.zeros((2 * SUBLANES, 2 * half), BF16)
    shift = lax.cond(jnp.max(bound) > SHIFT_SAFE, exact_shift, lambda: bound)
    qat_ref[B_HD:B_HD + 2 * SUBLANES, :] = jnp.where(shift_rows, -shift, 0.0).astype(BF16)

    acc_ref[...] = jnp.zeros_like(acc_ref)

    def attend(t, run):
        bias, run = bias_tile(t, run)
        s = _dot(kn_ref[t], qat_ref[...])
        for c0 in range(0, 2 * half, q):
            p_ref[:, c0:c0 + q] = jnp.exp(s[:, c0:c0 + q] + bias).astype(BF16)
        acc_ref[:half] += _dot_tn(p_ref[:, :half], va_ref[t])
        acc_ref[half:] += _dot_tn(p_ref[:, half:], vb_ref[t])
        return run

    lax.fori_loop(0, n_tiles, attend, run0)
    for j in range(B_HEADS // 2):
        a = acc_ref[j * q:(j + 1) * q]
        b = acc_ref[half + j * q:half + (j + 1) * q]
        o = jnp.where(lo_half, a / a[:, ONE_COL:ONE_COL + 1], b / b[:, 0:1])
        o_ref[:, j * LANES:(j + 1) * LANES] = o.astype(o_ref.dtype)


def _dsa(z3, q_gain, k_gain, *, tk, qrows):
    bsz, s, _ = z3.shape
    topk = min(TOPK_MAX, s // 4)
    assert tk >= topk >= 2 and s % tk == 0 and tk % LANES == 0 and s % qrows == 0 and qrows % LANES == 0
    groups = tk // (2 * SUBLANES)
    assert groups & (groups - 1) == 0 and s // (2 * SUBLANES) <= HALF_MAX
    n_t = s // tk
    w_q = B_HEADS * B_HD
    low = jnp.asarray(np.tril(np.ones((tk, tk), np.float32), -1), BF16)
    head_of = np.arange(w_q) // B_HD
    bd = jnp.asarray(head_of[:, None] == head_of[None, :], BF16)
    kg = jnp.concatenate([k_gain.astype(F32), jnp.ones((LANES - B_HD,), F32)]).reshape(1, LANES)
    qg = jnp.tile(q_gain.astype(F32) * (B_HD ** -0.5), B_HEADS).reshape(1, w_q)

    def const(a):
        return pl.BlockSpec(a.shape, lambda b, i: (0,) * a.ndim)

    return pl.pallas_call(
        functools.partial(_dsa_kernel, topk, tk),
        grid=(bsz, s // qrows),
        in_specs=[pl.BlockSpec((None, qrows, w_q), lambda b, i: (b, i, Z_BQ // w_q)),
                  pl.BlockSpec((None, qrows, w_q), lambda b, i: (b, i, Z_IQ // w_q)),
                  pl.BlockSpec((None, s, SM_W), lambda b, i: (b, 0, Z_SM // SM_W)),
                  const(qg), const(kg), const(bd), const(low)],
        out_specs=pl.BlockSpec((None, qrows, w_q), lambda b, i: (b, i, 0)),
        out_shape=jax.ShapeDtypeStruct((bsz, s, w_q), BF16),
        scratch_shapes=[pltpu.VMEM((n_t, tk, LANES), BF16),
                        pltpu.VMEM((n_t, tk, LANES), BF16),
                        pltpu.VMEM((n_t, tk, LANES), BF16),
                        pltpu.VMEM((n_t, tk, LANES), BF16),
                        pltpu.VMEM((SUBLANES, LANES), F32),
                        pltpu.VMEM((n_t, tk, qrows), jnp.int32),
                        pltpu.VMEM((n_t, tk, qrows), jnp.int16),
                        pltpu.VMEM((n_t, tk, qrows), jnp.int16),
                        pltpu.VMEM((LANES, B_HEADS * qrows), BF16),
                        pltpu.VMEM((LANES, IDX_HEADS * qrows), BF16),
                        pltpu.VMEM((tk, B_HEADS * qrows), BF16),
                        pltpu.VMEM((B_HEADS * qrows, LANES), F32)],
        compiler_params=_cparams(("arbitrary", "arbitrary")),
        name="dsa",
    )(z3, z3, z3, qg, kg, bd, low)


def _out_proj_kernel(x_ref, a_ref, b_ref, wa_ref, wb_ref, o_ref):
    o_ref[...] = x_ref[...] + _dot(a_ref[...], wa_ref[...]) + _dot(b_ref[...], wb_ref[...])


def _out_proj(x2d, a2d, b2d, w, *, tm, tn):
    m, d = x2d.shape
    ka = a2d.shape[1]
    kb = b2d.shape[1]
    return pl.pallas_call(
        _out_proj_kernel,
        grid=(m // tm, d // tn),
        in_specs=[pl.BlockSpec((tm, tn), lambda i, j: (i, j)),
                  pl.BlockSpec((tm, ka), lambda i, j: (i, 0)),
                  pl.BlockSpec((tm, kb), lambda i, j: (i, 0)),
                  pl.BlockSpec((ka, tn), lambda i, j: (0, j)),
                  pl.BlockSpec((kb, tn), lambda i, j: (0, j))],
        out_specs=pl.BlockSpec((tm, tn), lambda i, j: (i, j)),
        out_shape=jax.ShapeDtypeStruct((m, d), F32),
        compiler_params=_cparams(("arbitrary", "arbitrary")),
        name="even_out_proj",
    )(x2d, a2d, b2d, w[:ka], w[ka:])


def _causal_conv(u, prev, w_ref, b):
    width = w_ref.shape[0]
    rows = u.shape[0]

    def taps(x):
        y = w_ref[width - 1:width, :] * x + b
        for i in range(width - 1):
            y = y + w_ref[i:i + 1, :] * pltpu.roll(x, width - 1 - i, axis=0)
        return y

    head = taps(jnp.concatenate([prev, u[:SUBLANES]], axis=0))[SUBLANES:]
    return jnp.concatenate([head, taps(u)[SUBLANES:]], axis=0) if rows > SUBLANES else head


def _ffn_kernel(x_ref, g_ref, wg_ref, wv_ref, cwg_ref, cwv_ref, cbg_ref, cbv_ref, wd_ref, o_ref,
                xn_ref, acc_ref, pg_ref, pv_ref):
    s = pl.program_id(1)
    j = pl.program_id(2)
    rows = x_ref.shape[0]

    @pl.when(j == 0)
    def _():
        xn_ref[...] = _rms(x_ref[...], g_ref[...]).astype(BF16)
        acc_ref[...] = jnp.zeros_like(acc_ref)

    @pl.when(s == 0)
    def _():
        pg_ref[j] = jnp.zeros(pg_ref.shape[1:], F32)
        pv_ref[j] = jnp.zeros(pv_ref.shape[1:], F32)

    xn = xn_ref[...]

    def branch(w_ref, cw_ref, cb_ref, prev_ref):
        u = _dot(xn, w_ref[...])
        prev = prev_ref[j]
        prev_ref[j] = u[rows - SUBLANES:]
        return _causal_conv(u, prev, cw_ref, cb_ref[...])

    gate = branch(wg_ref, cwg_ref, cbg_ref, pg_ref)
    val = branch(wv_ref, cwv_ref, cbv_ref, pv_ref)
    act = _gelu_tanh(gate) * val
    acc_ref[...] += _dot(act.astype(BF16), wd_ref[...])

    @pl.when(j == pl.num_programs(2) - 1)
    def _():
        o_ref[...] = x_ref[...] + acc_ref[...]


def _conv_ffn(h3, gain, w_up, conv_w, conv_b, w_down, *, ts, tf):
    bsz, s, d = h3.shape
    dff = w_down.shape[0]
    nj = dff // tf
    width = conv_w.shape[0]
    cb = conv_b.reshape(1, -1)
    xspec = pl.BlockSpec((None, ts, d), lambda b, i, j: (b, i, 0))
    return pl.pallas_call(
        _ffn_kernel,
        grid=(bsz, s // ts, nj),
        in_specs=[xspec,
                  pl.BlockSpec((1, d), lambda b, i, j: (0, 0)),
                  pl.BlockSpec((d, tf), lambda b, i, j: (0, j)),
                  pl.BlockSpec((d, tf), lambda b, i, j: (0, j + nj)),
                  pl.BlockSpec((width, tf), lambda b, i, j: (0, j)),
                  pl.BlockSpec((width, tf), lambda b, i, j: (0, j + nj)),
                  pl.BlockSpec((1, tf), lambda b, i, j: (0, j)),
                  pl.BlockSpec((1, tf), lambda b, i, j: (0, j + nj)),
                  pl.BlockSpec((tf, d), lambda b, i, j: (j, 0))],
        out_specs=xspec,
        out_shape=jax.ShapeDtypeStruct((bsz, s, d), F32),
        scratch_shapes=[pltpu.VMEM((ts, d), BF16),
                        pltpu.VMEM((ts, d), F32),
                        pltpu.VMEM((nj, SUBLANES, tf), F32),
                        pltpu.VMEM((nj, SUBLANES, tf), F32)],
        compiler_params=_cparams(("arbitrary", "arbitrary", "arbitrary")),
        name="conv_ffn",
    )(h3, gain.reshape(1, d), w_up, w_up, conv_w, conv_w, cb, cb, w_down)


def _rglru_kernel(x_ref, g_ref, win_ref, cw_ref, cb_ref, wg_ref, ba_ref, bx_ref, lam_ref,
                  wout_ref, o_ref, a_ref, u_ref, y_ref, px_ref, ph_ref):
    rows = x_ref.shape[0]
    width = LRU_BLOCKS * LRU_BW

    @pl.when(pl.program_id(1) == 0)
    def _():
        px_ref[...] = jnp.zeros_like(px_ref)
        ph_ref[...] = jnp.zeros_like(ph_ref)

    xn = _rms(x_ref[...], g_ref[...]).astype(BF16)
    lam = lam_ref[...]
    log_sig = jnp.minimum(lam, 0.0) - jnp.log1p(jnp.exp(-jnp.abs(lam)))
    pair = 2 * LRU_BW
    for n2 in range(LRU_BLOCKS // 2):
        cs2 = slice(n2 * pair, (n2 + 1) * pair)
        y_ref[:, cs2] = _gelu_tanh(_dot(xn, win_ref[:, cs2]))
        xb = _dot(xn, win_ref[:, width + n2 * pair:width + (n2 + 1) * pair])
        prev = px_ref[:, cs2]
        px_ref[:, cs2] = xb[rows - SUBLANES:]
        xc2 = _causal_conv(xb, prev, cw_ref.at[:, cs2], cb_ref[:, cs2])
        for n in (2 * n2, 2 * n2 + 1):
            cs = slice(n * LRU_BW, (n + 1) * LRU_BW)
            xc = xc2[:, (n - 2 * n2) * LRU_BW:(n - 2 * n2 + 1) * LRU_BW]
            rg = _dot(xc.astype(BF16), wg_ref[n])
            r = _sigmoid(rg[:, :LRU_BW] + ba_ref[:, cs])
            gi = _sigmoid(rg[:, LRU_BW:] + bx_ref[:, cs])
            log_a = LRU_C * r * log_sig[:, cs]
            a = jnp.exp(log_a)
            a_ref[:, cs] = a
            u_ref[:, cs] = jnp.sqrt(-jnp.tanh(log_a) * (a * a + 1.0)) * (gi * xc)

    sub = lax.broadcasted_iota(jnp.int32, (SUBLANES, 1), 0)

    def group(gi_, h):
        r0 = pl.multiple_of(gi_ * SUBLANES, SUBLANES)
        a = a_ref[pl.ds(r0, SUBLANES), :]
        b = u_ref[pl.ds(r0, SUBLANES), :]
        d = 1
        while d < SUBLANES:
            keep = sub >= d
            b = jnp.where(keep, a * pltpu.roll(b, d, axis=0) + b, b)
            a = jnp.where(keep, a * pltpu.roll(a, d, axis=0), a)
            d *= 2
        hs = a * h + b
        u_ref[pl.ds(r0, SUBLANES), :] = hs
        return hs[SUBLANES - 1:, :]

    ph_ref[0:1, :] = lax.fori_loop(0, rows // SUBLANES, group, ph_ref[0:1, :], unroll=2)
    o_ref[...] = x_ref[...] + _dot((u_ref[...] * y_ref[...]).astype(BF16), wout_ref[...])


def _rglru(h3, gain, w_in, conv_w, conv_b, wa, ba, wx, bx, lam, w_out, *, ts):
    bsz, s, d = h3.shape
    width = LRU_BLOCKS * LRU_BW
    xspec = pl.BlockSpec((None, ts, d), lambda b, i: (b, i, 0))

    def full(a):
        return pl.BlockSpec(a.shape, lambda b, i: (0,) * a.ndim)

    wg = jnp.concatenate([wa, wx], axis=-1)
    args = (gain.reshape(1, d), w_in, conv_w, conv_b.reshape(1, width), wg, ba.reshape(1, width),
            bx.reshape(1, width), lam.reshape(1, width), w_out)
    return pl.pallas_call(
        _rglru_kernel,
        grid=(bsz, s // ts),
        in_specs=[xspec] + [full(a) for a in args],
        out_specs=xspec,
        out_shape=jax.ShapeDtypeStruct((bsz, s, d), F32),
        scratch_shapes=[pltpu.VMEM((ts, width), F32),
                        pltpu.VMEM((ts, width), F32),
                        pltpu.VMEM((ts, width), F32),
                        pltpu.VMEM((SUBLANES, width), F32),
                        pltpu.VMEM((SUBLANES, width), F32)],
        compiler_params=_cparams(("arbitrary", "arbitrary")),
        name="rglru_block",
    )(h3, *args)


def _tile(n, pref):
    t = min(n, pref)
    assert n % t == 0
    return t


def _tiles(bsz, s):
    return dict(
        proj_rows=_tile(bsz * s, 512),
        hgrn_rows=_tile(s, 512),
        hgrn_batch=_tile(bsz, 2),
        dsa_keys=_tile(s, 512),
        dsa_queries=_tile(s, 256),
        lru_rows=_tile(s, 512),
        ffn_rows=_tile(s, 1024),
        ffn_cols=1024,
    )


def kernel(x, lb_logits, even_norm, even_w_in, even_w_out, a_out_norm, b_q_norm, b_k_norm, odd_norm, odd_w_in, odd_conv_w, odd_conv_b, odd_gate_a_w, odd_gate_a_b, odd_gate_x_w, odd_gate_x_b, odd_lambda, odd_w_out, ffn_norm, ffn_w_up, ffn_conv_w, ffn_conv_b, ffn_w_down):
    bsz, s, d = x.shape
    depth = ffn_norm.shape[0]
    t = _tiles(bsz, s)
    h = x
    for layer in range(depth):
        j = layer // 2
        if layer % 2 == 0:
            w = even_w_in[j]
            o_bk = Z_IQ
            o_iq = o_bk + 2 * B_HD
            o_ik = o_iq + IDX_HEADS * IDX_DIM
            o_end = o_ik + IDX_DIM + IDX_HEADS
            assert w.shape[1] == o_end
            w = jnp.concatenate(
                [w[:, :o_bk], w[:, o_iq:o_ik], w[:, o_bk:o_iq], w[:, o_ik:o_end],
                 jnp.zeros((d, Z_W - o_end), w.dtype)], axis=1).astype(BF16)
            z = _norm_proj(h.reshape(bsz * s, d), even_norm[j], w,
                           tm=t["proj_rows"], tn=Z_W, out_dtype=BF16)
            z3 = z.reshape(bsz, s, Z_W)
            a_o = _hgrn(z3, lb_logits, a_out_norm[j], layer, t_rows=t["hgrn_rows"], b_rows=t["hgrn_batch"])
            b_o = _dsa(z3, b_q_norm[j], b_k_norm[j], tk=t["dsa_keys"], qrows=t["dsa_queries"])
            h = _out_proj(h.reshape(bsz * s, d), a_o.reshape(bsz * s, -1), b_o.reshape(bsz * s, -1),
                          even_w_out[j].astype(BF16), tm=t["proj_rows"], tn=d).reshape(bsz, s, d)
        else:
            h = _rglru(h, odd_norm[j], odd_w_in[j].astype(BF16), odd_conv_w[j], odd_conv_b[j],
                       odd_gate_a_w[j].astype(BF16), odd_gate_a_b[j], odd_gate_x_w[j].astype(BF16),
                       odd_gate_x_b[j], odd_lambda[j], odd_w_out[j].astype(BF16), ts=t["lru_rows"])
        h = _conv_ffn(h, ffn_norm[layer], ffn_w_up[layer].astype(BF16), ffn_conv_w[layer],
                      ffn_conv_b[layer], ffn_w_down[layer].astype(BF16), ts=t["ffn_rows"], tf=t["ffn_cols"])
    return h
```

```python
import functools

import numpy as np
import jax
import jax.numpy as jnp
from jax import lax
from jax.experimental import pallas as pl
from jax.experimental.pallas import tpu as pltpu

F32 = jnp.float32
BF16 = jnp.bfloat16

EPS = 1e-6
CHUNK = 64
A_HEADS = 4
A_D = 128
B_HEADS = 8
B_HD = 64
IDX_HEADS = 8
IDX_DIM = 64
TOPK_MAX = 256
LRU_BLOCKS = 10
LRU_BW = 128
LRU_C = 8.0
LANES = 128
SUBLANES = 8
V7X_VMEM_BYTES = 64 * 1024 * 1024
VMEM_LIMIT = V7X_VMEM_BYTES * 7 // 8

Z_A = 0
Z_BQ = Z_A + 4 * A_HEADS * A_D
Z_IQ = Z_BQ + B_HEADS * B_HD
Z_SM = Z_IQ + IDX_HEADS * IDX_DIM
SM_W = 2 * LANES
Z_W = Z_SM + SM_W
assert 2 * B_HD + IDX_DIM + IDX_HEADS <= SM_W


def _cparams(sem):
    return pltpu.CompilerParams(dimension_semantics=sem, vmem_limit_bytes=VMEM_LIMIT)


def _rms(x, gain):
    ms = jnp.mean(x * x, axis=-1, keepdims=True)
    return x * lax.rsqrt(ms + EPS) * gain


def _sigmoid(x):
    return 1.0 / (1.0 + jnp.exp(-x))


GELU_C = float(np.sqrt(2.0 / np.pi))
GELU_C3 = GELU_C * 0.044715


def _gelu_tanh(x):
    half = 0.5 * x
    return half + half * jnp.tanh(x * (GELU_C + GELU_C3 * (x * x)))


def _dot(a, b):
    return jnp.dot(a, b, preferred_element_type=F32)


def _dot_nt(a, b):
    return lax.dot_general(a, b, (((1,), (1,)), ((), ())), preferred_element_type=F32)


def _lane_tile(x, reps):
    return jnp.concatenate([x] * reps, axis=1)


def _dot_tn(a, b):
    return lax.dot_general(a, b, (((0,), (0,)), ((), ())), preferred_element_type=F32)


def _norm_proj_kernel(x_ref, g_ref, w_ref, o_ref, xn_ref):
    @pl.when(pl.program_id(1) == 0)
    def _():
        xn_ref[...] = _rms(x_ref[...], g_ref[...]).astype(BF16)

    o_ref[...] = _dot(xn_ref[...], w_ref[...]).astype(o_ref.dtype)


def _norm_proj(x2d, gain, w, *, tm, tn, out_dtype):
    m, d = x2d.shape
    n = w.shape[1]
    return pl.pallas_call(
        _norm_proj_kernel,
        grid=(m // tm, n // tn),
        in_specs=[pl.BlockSpec((tm, d), lambda i, j: (i, 0)),
                  pl.BlockSpec((1, d), lambda i, j: (0, 0)),
                  pl.BlockSpec((d, tn), lambda i, j: (0, j))],
        out_specs=pl.BlockSpec((tm, tn), lambda i, j: (i, j)),
        out_shape=jax.ShapeDtypeStruct((m, n), out_dtype),
        scratch_shapes=[pltpu.VMEM((tm, d), BF16)],
        compiler_params=_cparams(("arbitrary", "arbitrary")),
        name="even_in_proj",
    )(x2d, gain.reshape(1, d), w)


def _hgrn_consts(c):
    levels = int(np.log2(c))
    tri = np.tril(np.ones((c, c), np.float32))
    masks = []
    t = np.arange(c)
    for l in range(levels):
        h = c >> (l + 1)
        base = (t // (2 * h)) * (2 * h)
        second = (t % (2 * h)) >= h
        same = base[:, None] == base[None, :]
        masks.append((same & second[:, None] & (~second)[None, :]).astype(np.float32))
    masks.append(np.eye(c, dtype=np.float32))
    return tri, np.stack(masks, 0), levels


def _hgrn_kernel(layer, c, levels, q_ref, f_ref, i_ref, g_ref, lbl_ref, an_ref, tri_ref, msk_ref,
                 o_ref, st_ref, b_ref):
    @pl.when(pl.program_id(1) == 0)
    def _():
        st_ref[...] = jnp.zeros_like(st_ref)

    lg = lbl_ref[...]
    e = jnp.exp(lg - jnp.max(lg, axis=0, keepdims=True))
    p = e / jnp.sum(e, axis=0, keepdims=True)
    lb = jnp.sum(p[:layer + 1], axis=0, keepdims=True)
    an = an_ref[...]
    n_chunks = q_ref.shape[1] // c
    width = A_HEADS * A_D
    row_pos = lax.broadcasted_iota(jnp.int32, (c, 1), 0)

    def chunk_row(bb, r0):
        qr = q_ref[bb, pl.ds(r0, c), :].astype(F32)
        fr = f_ref[bb, pl.ds(r0, c), :].astype(F32)
        v = i_ref[bb, pl.ds(r0, c), :].astype(BF16)
        gr = g_ref[bb, pl.ds(r0, c), :].astype(F32)
        q = qr * _sigmoid(qr)
        f = lb + (1.0 - lb) * _sigmoid(fr)
        g = jnp.log(f)
        k = 1.0 - f
        g1 = g.astype(BF16)
        r1 = g - g1.astype(F32)
        g2 = r1.astype(BF16)
        g3 = (r1 - g2.astype(F32)).astype(BF16)
        rs = _dot(tri_ref[...], jnp.concatenate([g1, g2, g3], axis=1))
        b = rs[:, :width] + rs[:, width:2 * width] + rs[:, 2 * width:]
        b_ref[bb] = b

        def midpoint(l):
            h = c >> (l + 1)
            if 2 * h >= SUBLANES:
                return jnp.concatenate([jnp.broadcast_to(b_ref[bb, base + h - 1:base + h, :], (2 * h, width))
                                        for base in range(0, c, 2 * h)], axis=0)
            out = b
            for k_ in range(2 * h):
                if k_ != h - 1:
                    out = jnp.where(row_pos % (2 * h) == k_, pltpu.roll(b, (k_ - h + 1) % c, axis=0), out)
            return out

        hs = [slice(h * A_D, (h + 1) * A_D) for h in range(A_HEADS)]
        qb_, kb_ = q.astype(BF16), k.astype(BF16)
        scores = [msk_ref[levels] * _dot_nt(qb_[:, s_], kb_[:, s_]) for s_ in hs]
        for l in range(levels):
            x = jnp.exp(-jnp.abs(b - midpoint(l)))
            qx, kx = (q * x).astype(BF16), (k * x).astype(BF16)
            scores = [sc + msk_ref[l] * _dot_nt(qx[:, s_], kx[:, s_]) for sc, s_ in zip(scores, hs)]
        qe = (q * jnp.exp(b)).astype(BF16)
        b_last = b[c - 1:c, :]
        kd = (k * jnp.exp(b_last - b)).astype(BF16)
        decay = jnp.exp(b_last)
        gate = gr * _sigmoid(gr)
        for h, s_ in enumerate(hs):
            st = st_ref[bb, h]
            o = _dot_nt(qe[:, s_], st.astype(BF16)) + _dot(scores[h].astype(BF16), v[:, s_])
            st_ref[bb, h] = decay[:, s_] * st + _dot_tn(v[:, s_], kd[:, s_])
            o_ref[bb, pl.ds(r0, c), s_] = (_rms(o, an[:, s_]) * gate[:, s_]).astype(o_ref.dtype)

    def chunk(ci, carry):
        r0 = pl.multiple_of(ci * c, c)
        for bb in range(q_ref.shape[0]):
            chunk_row(bb, r0)
        return carry

    lax.fori_loop(0, n_chunks, chunk, 0)


def _hgrn(z3, lb_logits, a_norm, layer, *, t_rows, b_rows):
    bsz, s, _ = z3.shape
    tri, msk, levels = _hgrn_consts(CHUNK)
    tri = jnp.asarray(tri, BF16)
    msk = jnp.asarray(msk, F32)
    nl = lb_logits.shape[0]
    width = A_HEADS * A_D
    col = Z_A // width

    def zspec(k):
        return pl.BlockSpec((b_rows, t_rows, width), lambda b, t: (b, t, col + k))

    return pl.pallas_call(
        functools.partial(_hgrn_kernel, layer, CHUNK, levels),
        grid=(bsz // b_rows, s // t_rows),
        in_specs=[zspec(0), zspec(1), zspec(2), zspec(3),
                  pl.BlockSpec((nl, width), lambda b, t: (0, 0)),
                  pl.BlockSpec((1, width), lambda b, t: (0, 0)),
                  pl.BlockSpec(tri.shape, lambda b, t: (0, 0)),
                  pl.BlockSpec(msk.shape, lambda b, t: (0, 0, 0))],
        out_specs=pl.BlockSpec((b_rows, t_rows, width), lambda b, t: (b, t, 0)),
        out_shape=jax.ShapeDtypeStruct((bsz, s, width), BF16),
        scratch_shapes=[pltpu.VMEM((b_rows, A_HEADS, A_D, A_D), F32),
                        pltpu.VMEM((b_rows, CHUNK, width), F32)],
        compiler_params=_cparams(("arbitrary", "arbitrary")),
        name="hgrn2",
    )(z3, z3, z3, z3, lb_logits, a_norm.reshape(1, -1), tri, msk)


INT_MIN = -2 ** 31
KEY_MASK = 0x7FFFFFFF
HALF_BITS = 16
HALF_MASK = (1 << HALF_BITS) - 1
HALF_MIN = -(1 << (HALF_BITS - 1))
HALF_MAX = (1 << (HALF_BITS - 1)) - 1
MASKED = -1e30
SHIFT_SAFE = 30.0
ONE_COL = B_HD
COUNT_CHAINS = 4

def _sort_key(x):
    bits = pltpu.bitcast(x, jnp.int32)
    key = bits ^ ((bits >> 31) & KEY_MASK)
    return jnp.where(x == 0.0, 0, key)


def _head_col(h, q):
    return ((h % 2) * (B_HEADS // 2) + h // 2) * q


def _dsa_kernel(topk, tk, zq_ref, zi_ref, zs_ref, qg_ref, kg_ref, bd_ref, low_ref, o_ref,
                kn_ref, ki_ref, va_ref, vb_ref, kmax_ref, key_ref, khi_ref, klo_ref, qat_ref, qit_ref, p_ref,
                acc_ref):
    qb = pl.program_id(1)
    q = zq_ref.shape[0]
    half = (B_HEADS // 2) * q
    n_tiles_all = kn_ref.shape[0]
    lane = lax.broadcasted_iota(jnp.int32, (1, LANES), 1)
    lo_half = lane < B_HD
    one_col = jnp.where(lane == ONE_COL, 1.0, 0.0)

    @pl.when(qb == 0)
    def _():
        kg = kg_ref[...]
        qat_ref[...] = jnp.zeros_like(qat_ref)
        qit_ref[...] = jnp.zeros_like(qit_ref)

        def prep(t, kmax):
            r0 = pl.multiple_of(t * tk, tk)
            rows = zs_ref[pl.ds(r0, tk), :].astype(F32)
            kv = rows[:, :LANES]
            ms = jnp.sum(jnp.where(lo_half, kv * kv, 0.0), axis=-1, keepdims=True) * (1.0 / B_HD)
            kn = jnp.where(lo_half, kv * lax.rsqrt(ms + EPS) * kg, 0.0)
            kn_ref[t] = (kn + one_col).astype(BF16)
            ki_ref[t] = jnp.where(lo_half, rows[:, LANES:], 0.0).astype(BF16)
            va_ref[t] = jnp.where(lo_half, pltpu.roll(kv, B_HD, axis=1), 1.0).astype(BF16)
            vb_ref[t] = jnp.where(lo_half, 1.0, kv).astype(BF16)
            return jnp.maximum(kmax, jnp.max(jnp.sum(kn * kn, axis=-1, keepdims=True), axis=0, keepdims=True))

        kmax = lax.fori_loop(0, n_tiles_all, prep, jnp.zeros((1, 1), F32))
        kmax_ref[...] = jnp.broadcast_to(jnp.sqrt(kmax), kmax_ref.shape)

    qf = zq_ref[...].astype(F32)
    sq = qf * qf
    sq_hi = sq.astype(BF16)
    sq_lo = (sq - sq_hi.astype(F32)).astype(BF16)
    ssum = _dot(sq_hi, bd_ref[...]) + _dot(sq_lo, bd_ref[...])
    qnt = (qf * lax.rsqrt(ssum * (1.0 / B_HD) + EPS) * qg_ref[...]).T
    qit = zi_ref[...].astype(F32).T
    kmax_q = _lane_tile(kmax_ref[0:1, :], q // LANES)
    bound = []
    for h in range(B_HEADS):
        c0 = _head_col(h, q)
        blk = qnt[h * B_HD:(h + 1) * B_HD, :]
        qat_ref[0:B_HD, c0:c0 + q] = blk.astype(BF16)
        qit_ref[0:IDX_DIM, c0:c0 + q] = qit[h * IDX_DIM:(h + 1) * IDX_DIM, :].astype(BF16)
        bound.append(jnp.sqrt(jnp.sum(blk * blk, axis=0, keepdims=True)) * kmax_q)
    q0 = pl.multiple_of(qb * q, q)
    wt = zs_ref[pl.ds(q0, q), :].astype(F32)[:, LANES:].T
    wt = wt[IDX_DIM:IDX_DIM + IDX_HEADS, :] * ((IDX_HEADS ** -0.5) * (IDX_DIM ** -0.5))

    n_tiles = (q0 + q + tk - 1) // tk
    qpos = lax.broadcasted_iota(jnp.int32, (1, q), 1)
    limit = q0 + ((qpos // CHUNK) + 1) * CHUNK

    def index_tile(t, carry):
        ki = ki_ref[t]
        acc = jnp.zeros((tk, q), F32)
        for h in range(IDX_HEADS):
            c0 = _head_col(h, q)
            acc = acc + jnp.maximum(_dot(ki, qit_ref[:, c0:c0 + q]), 0.0) * wt[h:h + 1, :]
        kpos = t * tk + lax.broadcasted_iota(jnp.int32, (tk, 1), 0)
        key = jnp.where(kpos < limit, _sort_key(acc), INT_MIN)
        key_ref[t] = key
        khi_ref[t] = (key >> HALF_BITS).astype(jnp.int16)
        klo_ref[t] = ((key & HALF_MASK) + HALF_MIN).astype(jnp.int16)
        return carry

    lax.fori_loop(0, n_tiles, index_tile, 0)

    def count_ge(cand):
        def body(t, acc):
            hit = jnp.where(key_ref[t] >= cand, 1.0, 0.0)
            return acc + jnp.sum(hit.reshape(COUNT_CHAINS, tk // (COUNT_CHAINS * SUBLANES), SUBLANES, q), axis=1)
        acc = lax.fori_loop(0, n_tiles, body, jnp.zeros((COUNT_CHAINS, SUBLANES, q), F32))
        return jnp.sum(jnp.sum(acc, axis=0), axis=0, keepdims=True)

    one16 = jnp.ones((), jnp.int16)
    packed_rows = 2 * SUBLANES

    def count_ge16(cand):
        c16 = cand.astype(jnp.int16)

        def body(t, acc):
            x = jnp.where(khi_ref[t] >= c16, one16, 0 * one16).reshape(tk // packed_rows, packed_rows, q)
            while x.shape[0] > 1:
                x = x[:x.shape[0] // 2] + x[x.shape[0] // 2:]
            return acc + x[0]
        acc = lax.fori_loop(0, n_tiles, body, jnp.zeros((packed_rows, q), jnp.int16))
        return jnp.sum(acc.astype(jnp.int32), axis=0, keepdims=True)

    def half_search():
        def bit_step(i, top):
            cand = top + lax.shift_left(jnp.int32(1), HALF_BITS - 1 - i)
            return jnp.where(count_ge16(cand) >= topk, cand, top)
        return lax.fori_loop(0, HALF_BITS, bit_step, jnp.full((1, q), HALF_MIN, jnp.int32))

    top = half_search()
    top16 = top.astype(jnp.int16)

    def narrow(t, carry):
        hi = khi_ref[t]
        khi_ref[t] = jnp.where(hi > top16, HALF_MAX * one16, jnp.where(hi < top16, HALF_MIN * one16, klo_ref[t]))
        return carry

    lax.fori_loop(0, n_tiles, narrow, 0)
    thr = top * (1 << HALF_BITS) + (half_search() - HALF_MIN)
    need = float(topk) - count_ge(thr + 1)
    need = jnp.where(thr == INT_MIN, 0.0, need)

    def bias_tile(t, run):
        keys = key_ref[t]
        eq = keys == thr
        eqf = jnp.where(eq, 1.0, 0.0)
        rank = _dot(low_ref[...], eqf.astype(BF16)) + run
        tie_ok = jnp.where(eq, rank, float(topk)) < need
        bias = jnp.where(keys > thr, 0.0, jnp.where(tie_ok, 0.0, MASKED))
        return bias, run + jnp.sum(eqf, axis=0, keepdims=True)

    run0 = jnp.zeros((1, q), F32)
    bound = jnp.concatenate([bound[h] for h in sorted(range(B_HEADS), key=lambda h: _head_col(h, q))], axis=1)

    def exact_shift():
        def body(t, carry):
            run, mx = carry
            bias, run = bias_tile(t, run)
            s = _dot(kn_ref[t], qat_ref[...])
            cols = [jnp.max(s[:, c0:c0 + q] + bias, axis=0, keepdims=True) for c0 in range(0, 2 * half, q)]
            return run, jnp.maximum(mx, jnp.concatenate(cols, axis=1))
        return lax.fori_loop(0, n_tiles, body, (run0, jnp.full((1, 2 * half), MASKED, F32)))[1]

    shift_rows = lax.broadcasted_iota(jnp.int32, (2 * SUBLANES, 1), 0) == 0
    qat_ref[B_HD:B_HD + 2 * SUBLANES, :] = jnp.zeros((2 * SUBLANES, 2 * half), BF16)
    shift = lax.cond(jnp.max(bound) > SHIFT_SAFE, exact_shift, lambda: bound)
    qat_ref[B_HD:B_HD + 2 * SUBLANES, :] = jnp.where(shift_rows, -shift, 0.0).astype(BF16)

    acc_ref[...] = jnp.zeros_like(acc_ref)

    def attend(t, run):
        bias, run = bias_tile(t, run)
        kn = kn_ref[t]
        for c0 in range(0, 2 * half, q):
            p_ref[:, c0:c0 + q] = jnp.exp(_dot(kn, qat_ref[:, c0:c0 + q]) + bias).astype(BF16)
        acc_ref[:half] += _dot_tn(p_ref[:, :half], va_ref[t])
        acc_ref[half:] += _dot_tn(p_ref[:, half:], vb_ref[t])
        return run

    lax.fori_loop(0, n_tiles, attend, run0)
    for j in range(B_HEADS // 2):
        a = acc_ref[j * q:(j + 1) * q]
        b = acc_ref[half + j * q:half + (j + 1) * q]
        o = jnp.where(lo_half, a / pltpu.roll(a, B_HD, axis=1), b / pltpu.roll(b, B_HD, axis=1))
        o_ref[:, j * LANES:(j + 1) * LANES] = o.astype(o_ref.dtype)


def _dsa(z3, q_gain, k_gain, *, tk, qrows):
    bsz, s, _ = z3.shape
    topk = min(TOPK_MAX, s // 4)
    assert tk >= topk >= 2 and s % tk == 0 and tk % LANES == 0 and s % qrows == 0 and qrows % LANES == 0
    groups = tk // (2 * SUBLANES)
    assert groups & (groups - 1) == 0 and s // (2 * SUBLANES) <= HALF_MAX
    n_t = s // tk
    w_q = B_HEADS * B_HD
    low = jnp.asarray(np.tril(np.ones((tk, tk), np.float32), -1), BF16)
    head_of = np.arange(w_q) // B_HD
    bd = jnp.asarray(head_of[:, None] == head_of[None, :], BF16)
    kg = jnp.concatenate([k_gain.astype(F32), jnp.ones((LANES - B_HD,), F32)]).reshape(1, LANES)
    qg = jnp.tile(q_gain.astype(F32) * (B_HD ** -0.5), B_HEADS).reshape(1, w_q)

    def const(a):
        return pl.BlockSpec(a.shape, lambda b, i: (0,) * a.ndim)

    return pl.pallas_call(
        functools.partial(_dsa_kernel, topk, tk),
        grid=(bsz, s // qrows),
        in_specs=[pl.BlockSpec((None, qrows, w_q), lambda b, i: (b, i, Z_BQ // w_q)),
                  pl.BlockSpec((None, qrows, w_q), lambda b, i: (b, i, Z_IQ // w_q)),
                  pl.BlockSpec((None, s, SM_W), lambda b, i: (b, 0, Z_SM // SM_W)),
                  const(qg), const(kg), const(bd), const(low)],
        out_specs=pl.BlockSpec((None, qrows, w_q), lambda b, i: (b, i, 0)),
        out_shape=jax.ShapeDtypeStruct((bsz, s, w_q), BF16),
        scratch_shapes=[pltpu.VMEM((n_t, tk, LANES), BF16),
                        pltpu.VMEM((n_t, tk, LANES), BF16),
                        pltpu.VMEM((n_t, tk, LANES), BF16),
                        pltpu.VMEM((n_t, tk, LANES), BF16),
                        pltpu.VMEM((SUBLANES, LANES), F32),
                        pltpu.VMEM((n_t, tk, qrows), jnp.int32),
                        pltpu.VMEM((n_t, tk, qrows), jnp.int16),
                        pltpu.VMEM((n_t, tk, qrows), jnp.int16),
                        pltpu.VMEM((LANES, B_HEADS * qrows), BF16),
                        pltpu.VMEM((LANES, IDX_HEADS * qrows), BF16),
                        pltpu.VMEM((tk, B_HEADS * qrows), BF16),
                        pltpu.VMEM((B_HEADS * qrows, LANES), F32)],
        compiler_params=_cparams(("arbitrary", "arbitrary")),
        name="dsa",
    )(z3, z3, z3, qg, kg, bd, low)


def _out_proj_kernel(x_ref, a_ref, b_ref, wa_ref, wb_ref, o_ref):
    o_ref[...] = x_ref[...] + _dot(a_ref[...], wa_ref[...]) + _dot(b_ref[...], wb_ref[...])


def _out_proj(x2d, a2d, b2d, w, *, tm, tn):
    m, d = x2d.shape
    ka = a2d.shape[1]
    kb = b2d.shape[1]
    return pl.pallas_call(
        _out_proj_kernel,
        grid=(m // tm, d // tn),
        in_specs=[pl.BlockSpec((tm, tn), lambda i, j: (i, j)),
                  pl.BlockSpec((tm, ka), lambda i, j: (i, 0)),
                  pl.BlockSpec((tm, kb), lambda i, j: (i, 0)),
                  pl.BlockSpec((ka, tn), lambda i, j: (0, j)),
                  pl.BlockSpec((kb, tn), lambda i, j: (0, j))],
        out_specs=pl.BlockSpec((tm, tn), lambda i, j: (i, j)),
        out_shape=jax.ShapeDtypeStruct((m, d), F32),
        compiler_params=_cparams(("arbitrary", "arbitrary")),
        name="even_out_proj",
    )(x2d, a2d, b2d, w[:ka], w[ka:])


def _causal_conv(u, prev, w_ref, b):
    width = w_ref.shape[0]
    rows = u.shape[0]

    def taps(x):
        y = w_ref[width - 1:width, :] * x + b
        for i in range(width - 1):
            y = y + w_ref[i:i + 1, :] * pltpu.roll(x, width - 1 - i, axis=0)
        return y

    head = taps(jnp.concatenate([prev, u[:SUBLANES]], axis=0))[SUBLANES:]
    return jnp.concatenate([head, taps(u)[SUBLANES:]], axis=0) if rows > SUBLANES else head


def _ffn_kernel(x_ref, g_ref, wg_ref, wv_ref, cwg_ref, cwv_ref, cbg_ref, cbv_ref, wd_ref, o_ref,
                xn_ref, acc_ref, pg_ref, pv_ref):
    s = pl.program_id(1)
    j = pl.program_id(2)
    rows = x_ref.shape[0]

    @pl.when(j == 0)
    def _():
        xn_ref[...] = _rms(x_ref[...], g_ref[...]).astype(BF16)
        acc_ref[...] = jnp.zeros_like(acc_ref)

    @pl.when(s == 0)
    def _():
        pg_ref[j] = jnp.zeros(pg_ref.shape[1:], F32)
        pv_ref[j] = jnp.zeros(pv_ref.shape[1:], F32)

    xn = xn_ref[...]

    def branch(w_ref, cw_ref, cb_ref, prev_ref):
        u = _dot(xn, w_ref[...])
        prev = prev_ref[j]
        prev_ref[j] = u[rows - SUBLANES:]
        return _causal_conv(u, prev, cw_ref, cb_ref[...])

    gate = branch(wg_ref, cwg_ref, cbg_ref, pg_ref)
    val = branch(wv_ref, cwv_ref, cbv_ref, pv_ref)
    act = _gelu_tanh(gate) * val
    acc_ref[...] += _dot(act.astype(BF16), wd_ref[...])

    @pl.when(j == pl.num_programs(2) - 1)
    def _():
        o_ref[...] = x_ref[...] + acc_ref[...]


def _conv_ffn(h3, gain, w_up, conv_w, conv_b, w_down, *, ts, tf):
    bsz, s, d = h3.shape
    dff = w_down.shape[0]
    nj = dff // tf
    width = conv_w.shape[0]
    cb = conv_b.reshape(1, -1)
    xspec = pl.BlockSpec((None, ts, d), lambda b, i, j: (b, i, 0))
    return pl.pallas_call(
        _ffn_kernel,
        grid=(bsz, s // ts, nj),
        in_specs=[xspec,
                  pl.BlockSpec((1, d), lambda b, i, j: (0, 0)),
                  pl.BlockSpec((d, tf), lambda b, i, j: (0, j)),
                  pl.BlockSpec((d, tf), lambda b, i, j: (0, j + nj)),
                  pl.BlockSpec((width, tf), lambda b, i, j: (0, j)),
                  pl.BlockSpec((width, tf), lambda b, i, j: (0, j + nj)),
                  pl.BlockSpec((1, tf), lambda b, i, j: (0, j)),
                  pl.BlockSpec((1, tf), lambda b, i, j: (0, j + nj)),
                  pl.BlockSpec((tf, d), lambda b, i, j: (j, 0))],
        out_specs=xspec,
        out_shape=jax.ShapeDtypeStruct((bsz, s, d), F32),
        scratch_shapes=[pltpu.VMEM((ts, d), BF16),
                        pltpu.VMEM((ts, d), F32),
                        pltpu.VMEM((nj, SUBLANES, tf), F32),
                        pltpu.VMEM((nj, SUBLANES, tf), F32)],
        compiler_params=_cparams(("arbitrary", "arbitrary", "arbitrary")),
        name="conv_ffn",
    )(h3, gain.reshape(1, d), w_up, w_up, conv_w, conv_w, cb, cb, w_down)


def _rglru_kernel(x_ref, g_ref, win_ref, cw_ref, cb_ref, wg_ref, ba_ref, bx_ref, lam_ref,
                  wout_ref, o_ref, a_ref, u_ref, y_ref, px_ref, ph_ref):
    rows = x_ref.shape[0]
    width = LRU_BLOCKS * LRU_BW

    @pl.when(pl.program_id(1) == 0)
    def _():
        px_ref[...] = jnp.zeros_like(px_ref)
        ph_ref[...] = jnp.zeros_like(ph_ref)

    xn = _rms(x_ref[...], g_ref[...]).astype(BF16)
    lam = lam_ref[...]
    log_sig = jnp.minimum(lam, 0.0) - jnp.log1p(jnp.exp(-jnp.abs(lam)))
    pair = 2 * LRU_BW
    for n2 in range(LRU_BLOCKS // 2):
        cs2 = slice(n2 * pair, (n2 + 1) * pair)
        y_ref[:, cs2] = _gelu_tanh(_dot(xn, win_ref[:, cs2]))
        xb = _dot(xn, win_ref[:, width + n2 * pair:width + (n2 + 1) * pair])
        prev = px_ref[:, cs2]
        px_ref[:, cs2] = xb[rows - SUBLANES:]
        xc2 = _causal_conv(xb, prev, cw_ref.at[:, cs2], cb_ref[:, cs2])
        for n in (2 * n2, 2 * n2 + 1):
            cs = slice(n * LRU_BW, (n + 1) * LRU_BW)
            xc = xc2[:, (n - 2 * n2) * LRU_BW:(n - 2 * n2 + 1) * LRU_BW]
            rg = _dot(xc.astype(BF16), wg_ref[n])
            r = _sigmoid(rg[:, :LRU_BW] + ba_ref[:, cs])
            gi = _sigmoid(rg[:, LRU_BW:] + bx_ref[:, cs])
            log_a = LRU_C * r * log_sig[:, cs]
            a = jnp.exp(log_a)
            a_ref[:, cs] = a
            u_ref[:, cs] = jnp.sqrt(-jnp.tanh(log_a) * (a * a + 1.0)) * (gi * xc)

    sub = lax.broadcasted_iota(jnp.int32, (SUBLANES, 1), 0)

    def group(gi_, h):
        r0 = pl.multiple_of(gi_ * SUBLANES, SUBLANES)
        a = a_ref[pl.ds(r0, SUBLANES), :]
        b = u_ref[pl.ds(r0, SUBLANES), :]
        d = 1
        while d < SUBLANES:
            keep = sub >= d
            b = jnp.where(keep, a * pltpu.roll(b, d, axis=0) + b, b)
            a = jnp.where(keep, a * pltpu.roll(a, d, axis=0), a)
            d *= 2
        hs = a * h + b
        u_ref[pl.ds(r0, SUBLANES), :] = hs
        return hs[SUBLANES - 1:, :]

    ph_ref[0:1, :] = lax.fori_loop(0, rows // SUBLANES, group, ph_ref[0:1, :], unroll=2)
    o_ref[...] = x_ref[...] + _dot((u_ref[...] * y_ref[...]).astype(BF16), wout_ref[...])


def _rglru(h3, gain, w_in, conv_w, conv_b, wa, ba, wx, bx, lam, w_out, *, ts):
    bsz, s, d = h3.shape
    width = LRU_BLOCKS * LRU_BW
    xspec = pl.BlockSpec((None, ts, d), lambda b, i: (b, i, 0))

    def full(a):
        return pl.BlockSpec(a.shape, lambda b, i: (0,) * a.ndim)

    wg = jnp.concatenate([wa, wx], axis=-1)
    args = (gain.reshape(1, d), w_in, conv_w, conv_b.reshape(1, width), wg, ba.reshape(1, width),
            bx.reshape(1, width), lam.reshape(1, width), w_out)
    return pl.pallas_call(
        _rglru_kernel,
        grid=(bsz, s // ts),
        in_specs=[xspec] + [full(a) for a in args],
        out_specs=xspec,
        out_shape=jax.ShapeDtypeStruct((bsz, s, d), F32),
        scratch_shapes=[pltpu.VMEM((ts, width), F32),
                        pltpu.VMEM((ts, width), F32),
                        pltpu.VMEM((ts, width), F32),
                        pltpu.VMEM((SUBLANES, width), F32),
                        pltpu.VMEM((SUBLANES, width), F32)],
        compiler_params=_cparams(("arbitrary", "arbitrary")),
        name="rglru_block",
    )(h3, *args)


def _tile(n, pref):
    t = min(n, pref)
    assert n % t == 0
    return t


def _tiles(bsz, s):
    return dict(
        proj_rows=_tile(bsz * s, 512),
        hgrn_rows=_tile(s, 512),
        hgrn_batch=_tile(bsz, 2),
        dsa_keys=_tile(s, 512),
        dsa_queries=_tile(s, 256),
        lru_rows=_tile(s, 512),
        ffn_rows=_tile(s, 1024),
        ffn_cols=1024,
    )


def kernel(x, lb_logits, even_norm, even_w_in, even_w_out, a_out_norm, b_q_norm, b_k_norm, odd_norm, odd_w_in, odd_conv_w, odd_conv_b, odd_gate_a_w, odd_gate_a_b, odd_gate_x_w, odd_gate_x_b, odd_lambda, odd_w_out, ffn_norm, ffn_w_up, ffn_conv_w, ffn_conv_b, ffn_w_down):
    bsz, s, d = x.shape
    depth = ffn_norm.shape[0]
    t = _tiles(bsz, s)
    h = x
    for layer in range(depth):
        j = layer // 2
        if layer % 2 == 0:
            w = even_w_in[j]
            o_bk = Z_IQ
            o_iq = o_bk + 2 * B_HD
            o_ik = o_iq + IDX_HEADS * IDX_DIM
            o_end = o_ik + IDX_DIM + IDX_HEADS
            assert w.shape[1] == o_end
            w = jnp.concatenate(
                [w[:, :o_bk], w[:, o_iq:o_ik], w[:, o_bk:o_iq], w[:, o_ik:o_end],
                 jnp.zeros((d, Z_W - o_end), w.dtype)], axis=1).astype(BF16)
            z = _norm_proj(h.reshape(bsz * s, d), even_norm[j], w,
                           tm=t["proj_rows"], tn=Z_W, out_dtype=BF16)
            z3 = z.reshape(bsz, s, Z_W)
            a_o = _hgrn(z3, lb_logits, a_out_norm[j], layer, t_rows=t["hgrn_rows"], b_rows=t["hgrn_batch"])
            b_o = _dsa(z3, b_q_norm[j], b_k_norm[j], tk=t["dsa_keys"], qrows=t["dsa_queries"])
            h = _out_proj(h.reshape(bsz * s, d), a_o.reshape(bsz * s, -1), b_o.reshape(bsz * s, -1),
                          even_w_out[j].astype(BF16), tm=t["proj_rows"], tn=d).reshape(bsz, s, d)
        else:
            h = _rglru(h, odd_norm[j], odd_w_in[j].astype(BF16), odd_conv_w[j], odd_conv_b[j],
                       odd_gate_a_w[j].astype(BF16), odd_gate_a_b[j], odd_gate_x_w[j].astype(BF16),
                       odd_gate_x_b[j], odd_lambda[j], odd_w_out[j].astype(BF16), ts=t["lru_rows"])
        h = _conv_ffn(h, ffn_norm[layer], ffn_w_up[layer].astype(BF16), ffn_conv_w[layer],
                      ffn_conv_b[layer], ffn_w_down[layer].astype(BF16), ts=t["ffn_rows"], tf=t["ffn_cols"])
    return h
```

```python
import functools

import numpy as np
import jax
import jax.numpy as jnp
from jax import lax
from jax.experimental import pallas as pl
from jax.experimental.pallas import tpu as pltpu

F32 = jnp.float32
BF16 = jnp.bfloat16

EPS = 1e-6
CHUNK = 64
A_HEADS = 4
A_D = 128
B_HEADS = 8
B_HD = 64
IDX_HEADS = 8
IDX_DIM = 64
TOPK_MAX = 256
LRU_BLOCKS = 10
LRU_BW = 128
LRU_C = 8.0
LANES = 128
SUBLANES = 8
V7X_VMEM_BYTES = 64 * 1024 * 1024
VMEM_LIMIT = V7X_VMEM_BYTES * 7 // 8

Z_A = 0
Z_BQ = Z_A + 4 * A_HEADS * A_D
Z_IQ = Z_BQ + B_HEADS * B_HD
Z_SM = Z_IQ + IDX_HEADS * IDX_DIM
SM_W = 2 * LANES
Z_W = Z_SM + SM_W
assert 2 * B_HD + IDX_DIM + IDX_HEADS <= SM_W


def _cparams(sem):
    return pltpu.CompilerParams(dimension_semantics=sem, vmem_limit_bytes=VMEM_LIMIT)


def _rms(x, gain):
    ms = jnp.mean(x * x, axis=-1, keepdims=True)
    return x * lax.rsqrt(ms + EPS) * gain


def _sigmoid(x):
    return 1.0 / (1.0 + jnp.exp(-x))


GELU_C = float(np.sqrt(2.0 / np.pi))
GELU_C3 = GELU_C * 0.044715


def _gelu_tanh(x):
    half = 0.5 * x
    return half + half * jnp.tanh(x * (GELU_C + GELU_C3 * (x * x)))


def _dot(a, b):
    return jnp.dot(a, b, preferred_element_type=F32)


def _dot_nt(a, b):
    return lax.dot_general(a, b, (((1,), (1,)), ((), ())), preferred_element_type=F32)


def _lane_tile(x, reps):
    return jnp.concatenate([x] * reps, axis=1)


def _dot_tn(a, b):
    return lax.dot_general(a, b, (((0,), (0,)), ((), ())), preferred_element_type=F32)


def _norm_proj_kernel(x_ref, g_ref, w_ref, o_ref, xn_ref):
    @pl.when(pl.program_id(1) == 0)
    def _():
        xn_ref[...] = _rms(x_ref[...], g_ref[...]).astype(BF16)

    o_ref[...] = _dot(xn_ref[...], w_ref[...]).astype(o_ref.dtype)


def _norm_proj(x2d, gain, w, *, tm, tn, out_dtype):
    m, d = x2d.shape
    n = w.shape[1]
    return pl.pallas_call(
        _norm_proj_kernel,
        grid=(m // tm, n // tn),
        in_specs=[pl.BlockSpec((tm, d), lambda i, j: (i, 0)),
                  pl.BlockSpec((1, d), lambda i, j: (0, 0)),
                  pl.BlockSpec((d, tn), lambda i, j: (0, j))],
        out_specs=pl.BlockSpec((tm, tn), lambda i, j: (i, j)),
        out_shape=jax.ShapeDtypeStruct((m, n), out_dtype),
        scratch_shapes=[pltpu.VMEM((tm, d), BF16)],
        compiler_params=_cparams(("arbitrary", "arbitrary")),
        name="even_in_proj",
    )(x2d, gain.reshape(1, d), w)


def _hgrn_consts(c):
    levels = int(np.log2(c))
    tri = np.tril(np.ones((c, c), np.float32))
    masks = []
    t = np.arange(c)
    for l in range(levels):
        h = c >> (l + 1)
        base = (t // (2 * h)) * (2 * h)
        second = (t % (2 * h)) >= h
        same = base[:, None] == base[None, :]
        masks.append((same & second[:, None] & (~second)[None, :]).astype(np.float32))
    masks.append(np.eye(c, dtype=np.float32))
    return tri, np.stack(masks, 0), levels


def _hgrn_kernel(layer, c, levels, q_ref, f_ref, i_ref, g_ref, lbl_ref, an_ref, tri_ref, msk_ref,
                 o_ref, st_ref, b_ref):
    @pl.when(pl.program_id(1) == 0)
    def _():
        st_ref[...] = jnp.zeros_like(st_ref)

    lg = lbl_ref[...]
    e = jnp.exp(lg - jnp.max(lg, axis=0, keepdims=True))
    p = e / jnp.sum(e, axis=0, keepdims=True)
    lb = jnp.sum(p[:layer + 1], axis=0, keepdims=True)
    an = an_ref[...]
    n_chunks = q_ref.shape[1] // c
    width = A_HEADS * A_D
    row_pos = lax.broadcasted_iota(jnp.int32, (c, 1), 0)

    def chunk_row(bb, r0):
        qr = q_ref[bb, pl.ds(r0, c), :].astype(F32)
        fr = f_ref[bb, pl.ds(r0, c), :].astype(F32)
        v = i_ref[bb, pl.ds(r0, c), :].astype(BF16)
        gr = g_ref[bb, pl.ds(r0, c), :].astype(F32)
        q = qr * _sigmoid(qr)
        f = lb + (1.0 - lb) * _sigmoid(fr)
        g = jnp.log(f)
        k = 1.0 - f
        g1 = g.astype(BF16)
        r1 = g - g1.astype(F32)
        g2 = r1.astype(BF16)
        g3 = (r1 - g2.astype(F32)).astype(BF16)
        rs = _dot(tri_ref[...], jnp.concatenate([g1, g2, g3], axis=1))
        b = rs[:, :width] + rs[:, width:2 * width] + rs[:, 2 * width:]
        b_ref[bb] = b

        def midpoint(l):
            h = c >> (l + 1)
            if 2 * h >= SUBLANES:
                return jnp.concatenate([jnp.broadcast_to(b_ref[bb, base + h - 1:base + h, :], (2 * h, width))
                                        for base in range(0, c, 2 * h)], axis=0)
            out = b
            for k_ in range(2 * h):
                if k_ != h - 1:
                    out = jnp.where(row_pos % (2 * h) == k_, pltpu.roll(b, (k_ - h + 1) % c, axis=0), out)
            return out

        hs = [slice(h * A_D, (h + 1) * A_D) for h in range(A_HEADS)]
        qb_, kb_ = q.astype(BF16), k.astype(BF16)
        scores = [msk_ref[levels] * _dot_nt(qb_[:, s_], kb_[:, s_]) for s_ in hs]
        for l in range(levels):
            x = jnp.exp(-jnp.abs(b - midpoint(l)))
            qx, kx = (q * x).astype(BF16), (k * x).astype(BF16)
            scores = [sc + msk_ref[l] * _dot_nt(qx[:, s_], kx[:, s_]) for sc, s_ in zip(scores, hs)]
        qe = (q * jnp.exp(b)).astype(BF16)
        b_last = b[c - 1:c, :]
        kd = (k * jnp.exp(b_last - b)).astype(BF16)
        decay = jnp.exp(b_last)
        gate = gr * _sigmoid(gr)
        for h, s_ in enumerate(hs):
            st = st_ref[bb, h]
            o = _dot_nt(qe[:, s_], st.astype(BF16)) + _dot(scores[h].astype(BF16), v[:, s_])
            st_ref[bb, h] = decay[:, s_] * st + _dot_tn(v[:, s_], kd[:, s_])
            o_ref[bb, pl.ds(r0, c), s_] = (_rms(o, an[:, s_]) * gate[:, s_]).astype(o_ref.dtype)

    def chunk(ci, carry):
        r0 = pl.multiple_of(ci * c, c)
        for bb in range(q_ref.shape[0]):
            chunk_row(bb, r0)
        return carry

    lax.fori_loop(0, n_chunks, chunk, 0)


def _hgrn(z3, lb_logits, a_norm, layer, *, t_rows, b_rows):
    bsz, s, _ = z3.shape
    tri, msk, levels = _hgrn_consts(CHUNK)
    tri = jnp.asarray(tri, BF16)
    msk = jnp.asarray(msk, F32)
    nl = lb_logits.shape[0]
    width = A_HEADS * A_D
    col = Z_A // width

    def zspec(k):
        return pl.BlockSpec((b_rows, t_rows, width), lambda b, t: (b, t, col + k))

    return pl.pallas_call(
        functools.partial(_hgrn_kernel, layer, CHUNK, levels),
        grid=(bsz // b_rows, s // t_rows),
        in_specs=[zspec(0), zspec(1), zspec(2), zspec(3),
                  pl.BlockSpec((nl, width), lambda b, t: (0, 0)),
                  pl.BlockSpec((1, width), lambda b, t: (0, 0)),
                  pl.BlockSpec(tri.shape, lambda b, t: (0, 0)),
                  pl.BlockSpec(msk.shape, lambda b, t: (0, 0, 0))],
        out_specs=pl.BlockSpec((b_rows, t_rows, width), lambda b, t: (b, t, 0)),
        out_shape=jax.ShapeDtypeStruct((bsz, s, width), BF16),
        scratch_shapes=[pltpu.VMEM((b_rows, A_HEADS, A_D, A_D), F32),
                        pltpu.VMEM((b_rows, CHUNK, width), F32)],
        compiler_params=_cparams(("arbitrary", "arbitrary")),
        name="hgrn2",
    )(z3, z3, z3, z3, lb_logits, a_norm.reshape(1, -1), tri, msk)


INT_MIN = -2 ** 31
KEY_MASK = 0x7FFFFFFF
HALF_BITS = 16
HALF_MASK = (1 << HALF_BITS) - 1
HALF_MIN = -(1 << (HALF_BITS - 1))
HALF_MAX = (1 << (HALF_BITS - 1)) - 1
MASKED = -1e30
SHIFT_SAFE = 30.0
ONE_COL = B_HD
COUNT_CHAINS = 4

def _sort_key(x):
    bits = pltpu.bitcast(x, jnp.int32)
    key = bits ^ ((bits >> 31) & KEY_MASK)
    return jnp.where(x == 0.0, 0, key)


def _head_col(h, q):
    return ((h % 2) * (B_HEADS // 2) + h // 2) * q


def _dsa_kernel(topk, tk, zq_ref, zi_ref, zs_ref, qg_ref, kg_ref, bd_ref, low_ref, o_ref,
                kn_ref, ki_ref, va_ref, vb_ref, kmax_ref, key_ref, khi_ref, klo_ref, qat_ref, qit_ref, p_ref,
                acc_ref):
    qb = pl.program_id(1)
    q = zq_ref.shape[0]
    half = (B_HEADS // 2) * q
    n_tiles_all = kn_ref.shape[0]
    lane = lax.broadcasted_iota(jnp.int32, (1, LANES), 1)
    lo_half = lane < B_HD
    one_col = jnp.where(lane == ONE_COL, 1.0, 0.0)

    @pl.when(qb == 0)
    def _():
        kg = kg_ref[...]
        qat_ref[...] = jnp.zeros_like(qat_ref)
        qit_ref[...] = jnp.zeros_like(qit_ref)

        def prep(t, kmax):
            r0 = pl.multiple_of(t * tk, tk)
            rows = zs_ref[pl.ds(r0, tk), :].astype(F32)
            kv = rows[:, :LANES]
            ms = jnp.sum(jnp.where(lo_half, kv * kv, 0.0), axis=-1, keepdims=True) * (1.0 / B_HD)
            kn = jnp.where(lo_half, kv * lax.rsqrt(ms + EPS) * kg, 0.0)
            kn_ref[t] = (kn + one_col).astype(BF16)
            ki_ref[t] = jnp.where(lo_half, rows[:, LANES:], 0.0).astype(BF16)
            va_ref[t] = jnp.where(lo_half, pltpu.roll(kv, B_HD, axis=1), 1.0).astype(BF16)
            vb_ref[t] = jnp.where(lo_half, 1.0, kv).astype(BF16)
            return jnp.maximum(kmax, jnp.max(jnp.sum(kn * kn, axis=-1, keepdims=True), axis=0, keepdims=True))

        kmax = lax.fori_loop(0, n_tiles_all, prep, jnp.zeros((1, 1), F32))
        kmax_ref[...] = jnp.broadcast_to(jnp.sqrt(kmax), kmax_ref.shape)

    qf = zq_ref[...].astype(F32)
    sq = qf * qf
    sq_hi = sq.astype(BF16)
    sq_lo = (sq - sq_hi.astype(F32)).astype(BF16)
    ssum = _dot(sq_hi, bd_ref[...]) + _dot(sq_lo, bd_ref[...])
    qnt = (qf * lax.rsqrt(ssum * (1.0 / B_HD) + EPS) * qg_ref[...]).T
    qit = zi_ref[...].astype(F32).T
    kmax_q = _lane_tile(kmax_ref[0:1, :], q // LANES)
    bound = []
    for h in range(B_HEADS):
        c0 = _head_col(h, q)
        blk = qnt[h * B_HD:(h + 1) * B_HD, :]
        qat_ref[0:B_HD, c0:c0 + q] = blk.astype(BF16)
        qit_ref[0:IDX_DIM, c0:c0 + q] = qit[h * IDX_DIM:(h + 1) * IDX_DIM, :].astype(BF16)
        bound.append(jnp.sqrt(jnp.sum(blk * blk, axis=0, keepdims=True)) * kmax_q)
    q0 = pl.multiple_of(qb * q, q)
    wt = zs_ref[pl.ds(q0, q), :].astype(F32)[:, LANES:].T
    wt = wt[IDX_DIM:IDX_DIM + IDX_HEADS, :] * ((IDX_HEADS ** -0.5) * (IDX_DIM ** -0.5))

    n_tiles = (q0 + q + tk - 1) // tk
    qpos = lax.broadcasted_iota(jnp.int32, (1, q), 1)
    limit = q0 + ((qpos // CHUNK) + 1) * CHUNK

    def index_tile(t, carry):
        ki = ki_ref[t]
        acc = jnp.zeros((tk, q), F32)
        for h in range(IDX_HEADS):
            c0 = _head_col(h, q)
            acc = acc + jnp.maximum(_dot(ki, qit_ref[:, c0:c0 + q]), 0.0) * wt[h:h + 1, :]
        kpos = t * tk + lax.broadcasted_iota(jnp.int32, (tk, 1), 0)
        key = jnp.where(kpos < limit, _sort_key(acc), INT_MIN)
        key_ref[t] = key
        khi_ref[t] = (key >> HALF_BITS).astype(jnp.int16)
        klo_ref[t] = ((key & HALF_MASK) + HALF_MIN).astype(jnp.int16)
        return carry

    lax.fori_loop(0, n_tiles, index_tile, 0)

    def count_ge(cand):
        def body(t, acc):
            hit = jnp.where(key_ref[t] >= cand, 1.0, 0.0)
            return acc + jnp.sum(hit.reshape(COUNT_CHAINS, tk // (COUNT_CHAINS * SUBLANES), SUBLANES, q), axis=1)
        acc = lax.fori_loop(0, n_tiles, body, jnp.zeros((COUNT_CHAINS, SUBLANES, q), F32))
        return jnp.sum(jnp.sum(acc, axis=0), axis=0, keepdims=True)

    one16 = jnp.ones((), jnp.int16)
    packed_rows = 2 * SUBLANES

    def count_ge16(cand):
        c16 = cand.astype(jnp.int16)

        def body(t, acc):
            x = jnp.where(khi_ref[t] >= c16, one16, 0 * one16).reshape(tk // packed_rows, packed_rows, q)
            while x.shape[0] > 1:
                x = x[:x.shape[0] // 2] + x[x.shape[0] // 2:]
            return acc + x[0]
        acc = lax.fori_loop(0, n_tiles, body, jnp.zeros((packed_rows, q), jnp.int16))
        return jnp.sum(acc.astype(jnp.int32), axis=0, keepdims=True)

    def half_search():
        def bit_step(i, top):
            cand = top + lax.shift_left(jnp.int32(1), HALF_BITS - 1 - i)
            return jnp.where(count_ge16(cand) >= topk, cand, top)
        return lax.fori_loop(0, HALF_BITS, bit_step, jnp.full((1, q), HALF_MIN, jnp.int32))

    top = half_search()
    top16 = top.astype(jnp.int16)

    def narrow(t, carry):
        hi = khi_ref[t]
        khi_ref[t] = jnp.where(hi > top16, HALF_MAX * one16, jnp.where(hi < top16, HALF_MIN * one16, klo_ref[t]))
        return carry

    lax.fori_loop(0, n_tiles, narrow, 0)
    thr = top * (1 << HALF_BITS) + (half_search() - HALF_MIN)
    need = float(topk) - count_ge(thr + 1)
    need = jnp.where(thr == INT_MIN, 0.0, need)

    def bias_tile(t, run):
        keys = key_ref[t]
        eq = keys == thr
        eqf = jnp.where(eq, 1.0, 0.0)
        rank = _dot(low_ref[...], eqf.astype(BF16)) + run
        tie_ok = jnp.where(eq, rank, float(topk)) < need
        bias = jnp.where(keys > thr, 0.0, jnp.where(tie_ok, 0.0, MASKED))
        return bias, run + jnp.sum(eqf, axis=0, keepdims=True)

    run0 = jnp.zeros((1, q), F32)
    bound = jnp.concatenate([bound[h] for h in sorted(range(B_HEADS), key=lambda h: _head_col(h, q))], axis=1)

    def exact_shift():
        def body(t, carry):
            run, mx = carry
            bias, run = bias_tile(t, run)
            s = _dot(kn_ref[t], qat_ref[...])
            cols = [jnp.max(s[:, c0:c0 + q] + bias, axis=0, keepdims=True) for c0 in range(0, 2 * half, q)]
            return run, jnp.maximum(mx, jnp.concatenate(cols, axis=1))
        return lax.fori_loop(0, n_tiles, body, (run0, jnp.full((1, 2 * half), MASKED, F32)))[1]

    shift_rows = lax.broadcasted_iota(jnp.int32, (2 * SUBLANES, 1), 0) == 0
    qat_ref[B_HD:B_HD + 2 * SUBLANES, :] = jnp.zeros((2 * SUBLANES, 2 * half), BF16)
    shift = lax.cond(jnp.max(bound) > SHIFT_SAFE, exact_shift, lambda: bound)
    qat_ref[B_HD:B_HD + 2 * SUBLANES, :] = jnp.where(shift_rows, -shift, 0.0).astype(BF16)

    acc_ref[...] = jnp.zeros_like(acc_ref)

    def attend(t, run):
        bias, run = bias_tile(t, run)
        kn = kn_ref[t]
        for c0 in range(0, 2 * half, q):
            p_ref[:, c0:c0 + q] = jnp.exp(_dot(kn, qat_ref[:, c0:c0 + q]) + bias).astype(BF16)
        acc_ref[:half] += _dot_tn(p_ref[:, :half], va_ref[t])
        acc_ref[half:] += _dot_tn(p_ref[:, half:], vb_ref[t])
        return run

    lax.fori_loop(0, n_tiles, attend, run0)
    for j in range(B_HEADS // 2):
        a = acc_ref[j * q:(j + 1) * q]
        b = acc_ref[half + j * q:half + (j + 1) * q]
        o = jnp.where(lo_half, a / pltpu.roll(a, B_HD, axis=1), b / pltpu.roll(b, B_HD, axis=1))
        o_ref[:, j * LANES:(j + 1) * LANES] = o.astype(o_ref.dtype)


def _dsa(z3, q_gain, k_gain, *, tk, qrows):
    bsz, s, _ = z3.shape
    topk = min(TOPK_MAX, s // 4)
    assert tk >= topk >= 2 and s % tk == 0 and tk % LANES == 0 and s % qrows == 0 and qrows % LANES == 0
    groups = tk // (2 * SUBLANES)
    assert groups & (groups - 1) == 0 and s // (2 * SUBLANES) <= HALF_MAX
    n_t = s // tk
    w_q = B_HEADS * B_HD
    low = jnp.asarray(np.tril(np.ones((tk, tk), np.float32), -1), BF16)
    head_of = np.arange(w_q) // B_HD
    bd = jnp.asarray(head_of[:, None] == head_of[None, :], BF16)
    kg = jnp.concatenate([k_gain.astype(F32), jnp.ones((LANES - B_HD,), F32)]).reshape(1, LANES)
    qg = jnp.tile(q_gain.astype(F32) * (B_HD ** -0.5), B_HEADS).reshape(1, w_q)

    def const(a):
        return pl.BlockSpec(a.shape, lambda b, i: (0,) * a.ndim)

    return pl.pallas_call(
        functools.partial(_dsa_kernel, topk, tk),
        grid=(bsz, s // qrows),
        in_specs=[pl.BlockSpec((None, qrows, w_q), lambda b, i: (b, i, Z_BQ // w_q)),
                  pl.BlockSpec((None, qrows, w_q), lambda b, i: (b, i, Z_IQ // w_q)),
                  pl.BlockSpec((None, s, SM_W), lambda b, i: (b, 0, Z_SM // SM_W)),
                  const(qg), const(kg), const(bd), const(low)],
        out_specs=pl.BlockSpec((None, qrows, w_q), lambda b, i: (b, i, 0)),
        out_shape=jax.ShapeDtypeStruct((bsz, s, w_q), BF16),
        scratch_shapes=[pltpu.VMEM((n_t, tk, LANES), BF16),
                        pltpu.VMEM((n_t, tk, LANES), BF16),
                        pltpu.VMEM((n_t, tk, LANES), BF16),
                        pltpu.VMEM((n_t, tk, LANES), BF16),
                        pltpu.VMEM((SUBLANES, LANES), F32),
                        pltpu.VMEM((n_t, tk, qrows), jnp.int32),
                        pltpu.VMEM((n_t, tk, qrows), jnp.int16),
                        pltpu.VMEM((n_t, tk, qrows), jnp.int16),
                        pltpu.VMEM((LANES, B_HEADS * qrows), BF16),
                        pltpu.VMEM((LANES, IDX_HEADS * qrows), BF16),
                        pltpu.VMEM((tk, B_HEADS * qrows), BF16),
                        pltpu.VMEM((B_HEADS * qrows, LANES), F32)],
        compiler_params=_cparams(("arbitrary", "arbitrary")),
        name="dsa",
    )(z3, z3, z3, qg, kg, bd, low)


def _out_proj_kernel(x_ref, a_ref, b_ref, wa_ref, wb_ref, o_ref):
    o_ref[...] = x_ref[...] + _dot(a_ref[...], wa_ref[...]) + _dot(b_ref[...], wb_ref[...])


def _out_proj(x2d, a2d, b2d, w, *, tm, tn):
    m, d = x2d.shape
    ka = a2d.shape[1]
    kb = b2d.shape[1]
    return pl.pallas_call(
        _out_proj_kernel,
        grid=(m // tm, d // tn),
        in_specs=[pl.BlockSpec((tm, tn), lambda i, j: (i, j)),
                  pl.BlockSpec((tm, ka), lambda i, j: (i, 0)),
                  pl.BlockSpec((tm, kb), lambda i, j: (i, 0)),
                  pl.BlockSpec((ka, tn), lambda i, j: (0, j)),
                  pl.BlockSpec((kb, tn), lambda i, j: (0, j))],
        out_specs=pl.BlockSpec((tm, tn), lambda i, j: (i, j)),
        out_shape=jax.ShapeDtypeStruct((m, d), F32),
        compiler_params=_cparams(("arbitrary", "arbitrary")),
        name="even_out_proj",
    )(x2d, a2d, b2d, w[:ka], w[ka:])


def _causal_conv(u, prev, w_ref, b):
    width = w_ref.shape[0]
    rows = u.shape[0]

    def taps(x):
        y = w_ref[width - 1:width, :] * x + b
        for i in range(width - 1):
            y = y + w_ref[i:i + 1, :] * pltpu.roll(x, width - 1 - i, axis=0)
        return y

    head = taps(jnp.concatenate([prev, u[:SUBLANES]], axis=0))[SUBLANES:]
    return jnp.concatenate([head, taps(u)[SUBLANES:]], axis=0) if rows > SUBLANES else head


def _ffn_kernel(x_ref, g_ref, wg_ref, wv_ref, cwg_ref, cwv_ref, cbg_ref, cbv_ref, wd_ref, o_ref,
                xn_ref, pg_ref, pv_ref):
    s = pl.program_id(1)
    j = pl.program_id(2)
    rows = x_ref.shape[0]

    @pl.when(j == 0)
    def _():
        xn_ref[...] = _rms(x_ref[...], g_ref[...]).astype(BF16)
        o_ref[...] = x_ref[...]

    @pl.when(s == 0)
    def _():
        pg_ref[j] = jnp.zeros(pg_ref.shape[1:], F32)
        pv_ref[j] = jnp.zeros(pv_ref.shape[1:], F32)

    xn = xn_ref[...]

    def branch(w_ref, cw_ref, cb_ref, prev_ref):
        u = _dot(xn, w_ref[...])
        prev = prev_ref[j]
        prev_ref[j] = u[rows - SUBLANES:]
        return _causal_conv(u, prev, cw_ref, cb_ref[...])

    gate = branch(wg_ref, cwg_ref, cbg_ref, pg_ref)
    val = branch(wv_ref, cwv_ref, cbv_ref, pv_ref)
    act = _gelu_tanh(gate) * val
    o_ref[...] += _dot(act.astype(BF16), wd_ref[...])


def _conv_ffn(h3, gain, w_up, conv_w, conv_b, w_down, *, ts, tf):
    bsz, s, d = h3.shape
    dff = w_down.shape[0]
    nj = dff // tf
    width = conv_w.shape[0]
    cb = conv_b.reshape(1, -1)
    xspec = pl.BlockSpec((None, ts, d), lambda b, i, j: (b, i, 0))
    return pl.pallas_call(
        _ffn_kernel,
        grid=(bsz, s // ts, nj),
        in_specs=[xspec,
                  pl.BlockSpec((1, d), lambda b, i, j: (0, 0)),
                  pl.BlockSpec((d, tf), lambda b, i, j: (0, j)),
                  pl.BlockSpec((d, tf), lambda b, i, j: (0, j + nj)),
                  pl.BlockSpec((width, tf), lambda b, i, j: (0, j)),
                  pl.BlockSpec((width, tf), lambda b, i, j: (0, j + nj)),
                  pl.BlockSpec((1, tf), lambda b, i, j: (0, j)),
                  pl.BlockSpec((1, tf), lambda b, i, j: (0, j + nj)),
                  pl.BlockSpec((tf, d), lambda b, i, j: (j, 0))],
        out_specs=xspec,
        out_shape=jax.ShapeDtypeStruct((bsz, s, d), F32),
        scratch_shapes=[pltpu.VMEM((ts, d), BF16),
                        pltpu.VMEM((nj, SUBLANES, tf), F32),
                        pltpu.VMEM((nj, SUBLANES, tf), F32)],
        compiler_params=_cparams(("arbitrary", "arbitrary", "arbitrary")),
        name="conv_ffn",
    )(h3, gain.reshape(1, d), w_up, w_up, conv_w, conv_w, cb, cb, w_down)


def _rglru_kernel(x_ref, g_ref, win_ref, cw_ref, cb_ref, wg_ref, ba_ref, bx_ref, lam_ref,
                  wout_ref, o_ref, a_ref, u_ref, y_ref, px_ref, ph_ref):
    rows = x_ref.shape[0]
    width = LRU_BLOCKS * LRU_BW

    @pl.when(pl.program_id(1) == 0)
    def _():
        px_ref[...] = jnp.zeros_like(px_ref)
        ph_ref[...] = jnp.zeros_like(ph_ref)

    xn = _rms(x_ref[...], g_ref[...]).astype(BF16)
    lam = lam_ref[...]
    log_sig = jnp.minimum(lam, 0.0) - jnp.log1p(jnp.exp(-jnp.abs(lam)))
    pair = 2 * LRU_BW
    for n2 in range(LRU_BLOCKS // 2):
        cs2 = slice(n2 * pair, (n2 + 1) * pair)
        y_ref[:, cs2] = _gelu_tanh(_dot(xn, win_ref[:, cs2]))
        xb = _dot(xn, win_ref[:, width + n2 * pair:width + (n2 + 1) * pair])
        prev = px_ref[:, cs2]
        px_ref[:, cs2] = xb[rows - SUBLANES:]
        xc2 = _causal_conv(xb, prev, cw_ref.at[:, cs2], cb_ref[:, cs2])
        for n in (2 * n2, 2 * n2 + 1):
            cs = slice(n * LRU_BW, (n + 1) * LRU_BW)
            xc = xc2[:, (n - 2 * n2) * LRU_BW:(n - 2 * n2 + 1) * LRU_BW]
            rg = _dot(xc.astype(BF16), wg_ref[n])
            r = _sigmoid(rg[:, :LRU_BW] + ba_ref[:, cs])
            gi = _sigmoid(rg[:, LRU_BW:] + bx_ref[:, cs])
            log_a = LRU_C * r * log_sig[:, cs]
            a = jnp.exp(log_a)
            a_ref[:, cs] = a
            u_ref[:, cs] = jnp.sqrt(-jnp.tanh(log_a) * (a * a + 1.0)) * (gi * xc)

    sub = lax.broadcasted_iota(jnp.int32, (SUBLANES, 1), 0)

    def group(gi_, h):
        r0 = pl.multiple_of(gi_ * SUBLANES, SUBLANES)
        a = a_ref[pl.ds(r0, SUBLANES), :]
        b = u_ref[pl.ds(r0, SUBLANES), :]
        d = 1
        while d < SUBLANES:
            keep = sub >= d
            b = jnp.where(keep, a * pltpu.roll(b, d, axis=0) + b, b)
            a = jnp.where(keep, a * pltpu.roll(a, d, axis=0), a)
            d *= 2
        hs = a * h + b
        u_ref[pl.ds(r0, SUBLANES), :] = hs
        return hs[SUBLANES - 1:, :]

    ph_ref[0:1, :] = lax.fori_loop(0, rows // SUBLANES, group, ph_ref[0:1, :], unroll=2)
    o_ref[...] = x_ref[...] + _dot((u_ref[...] * y_ref[...]).astype(BF16), wout_ref[...])


def _rglru(h3, gain, w_in, conv_w, conv_b, wa, ba, wx, bx, lam, w_out, *, ts):
    bsz, s, d = h3.shape
    width = LRU_BLOCKS * LRU_BW
    xspec = pl.BlockSpec((None, ts, d), lambda b, i: (b, i, 0))

    def full(a):
        return pl.BlockSpec(a.shape, lambda b, i: (0,) * a.ndim)

    wg = jnp.concatenate([wa, wx], axis=-1)
    args = (gain.reshape(1, d), w_in, conv_w, conv_b.reshape(1, width), wg, ba.reshape(1, width),
            bx.reshape(1, width), lam.reshape(1, width), w_out)
    return pl.pallas_call(
        _rglru_kernel,
        grid=(bsz, s // ts),
        in_specs=[xspec] + [full(a) for a in args],
        out_specs=xspec,
        out_shape=jax.ShapeDtypeStruct((bsz, s, d), F32),
        scratch_shapes=[pltpu.VMEM((ts, width), F32),
                        pltpu.VMEM((ts, width), F32),
                        pltpu.VMEM((ts, width), F32),
                        pltpu.VMEM((SUBLANES, width), F32),
                        pltpu.VMEM((SUBLANES, width), F32)],
        compiler_params=_cparams(("arbitrary", "arbitrary")),
        name="rglru_block",
    )(h3, *args)


def _tile(n, pref):
    t = min(n, pref)
    assert n % t == 0
    return t


def _tiles(bsz, s):
    return dict(
        proj_rows=_tile(bsz * s, 512),
        hgrn_rows=_tile(s, 512),
        hgrn_batch=_tile(bsz, 2),
        dsa_keys=_tile(s, 512),
        dsa_queries=_tile(s, 256),
        lru_rows=_tile(s, 512),
        ffn_rows=_tile(s, 1024),
        ffn_cols=1024,
    )


def kernel(x, lb_logits, even_norm, even_w_in, even_w_out, a_out_norm, b_q_norm, b_k_norm, odd_norm, odd_w_in, odd_conv_w, odd_conv_b, odd_gate_a_w, odd_gate_a_b, odd_gate_x_w, odd_gate_x_b, odd_lambda, odd_w_out, ffn_norm, ffn_w_up, ffn_conv_w, ffn_conv_b, ffn_w_down):
    bsz, s, d = x.shape
    depth = ffn_norm.shape[0]
    t = _tiles(bsz, s)
    h = x
    for layer in range(depth):
        j = layer // 2
        if layer % 2 == 0:
            w = even_w_in[j]
            o_bk = Z_IQ
            o_iq = o_bk + 2 * B_HD
            o_ik = o_iq + IDX_HEADS * IDX_DIM
            o_end = o_ik + IDX_DIM + IDX_HEADS
            assert w.shape[1] == o_end
            w = jnp.concatenate(
                [w[:, :o_bk], w[:, o_iq:o_ik], w[:, o_bk:o_iq], w[:, o_ik:o_end],
                 jnp.zeros((d, Z_W - o_end), w.dtype)], axis=1).astype(BF16)
            z = _norm_proj(h.reshape(bsz * s, d), even_norm[j], w,
                           tm=t["proj_rows"], tn=Z_W, out_dtype=BF16)
            z3 = z.reshape(bsz, s, Z_W)
            a_o = _hgrn(z3, lb_logits, a_out_norm[j], layer, t_rows=t["hgrn_rows"], b_rows=t["hgrn_batch"])
            b_o = _dsa(z3, b_q_norm[j], b_k_norm[j], tk=t["dsa_keys"], qrows=t["dsa_queries"])
            h = _out_proj(h.reshape(bsz * s, d), a_o.reshape(bsz * s, -1), b_o.reshape(bsz * s, -1),
                          even_w_out[j].astype(BF16), tm=t["proj_rows"], tn=d).reshape(bsz, s, d)
        else:
            h = _rglru(h, odd_norm[j], odd_w_in[j].astype(BF16), odd_conv_w[j], odd_conv_b[j],
                       odd_gate_a_w[j].astype(BF16), odd_gate_a_b[j], odd_gate_x_w[j].astype(BF16),
                       odd_gate_x_b[j], odd_lambda[j], odd_w_out[j].astype(BF16), ts=t["lru_rows"])
        h = _conv_ffn(h, ffn_norm[layer], ffn_w_up[layer].astype(BF16), ffn_conv_w[layer],
                      ffn_conv_b[layer], ffn_w_down[layer].astype(BF16), ts=t["ffn_rows"], tf=t["ffn_cols"])
    return h
```

```python
import functools

import numpy as np
import jax
import jax.numpy as jnp
from jax import lax
from jax.experimental import pallas as pl
from jax.experimental.pallas import tpu as pltpu

F32 = jnp.float32
BF16 = jnp.bfloat16

EPS = 1e-6
CHUNK = 64
A_HEADS = 4
A_D = 128
B_HEADS = 8
B_HD = 64
IDX_HEADS = 8
IDX_DIM = 64
TOPK_MAX = 256
LRU_BLOCKS = 10
LRU_BW = 128
LRU_C = 8.0
LANES = 128
SUBLANES = 8
V7X_VMEM_BYTES = 64 * 1024 * 1024
VMEM_LIMIT = V7X_VMEM_BYTES * 7 // 8

Z_A = 0
Z_BQ = Z_A + 4 * A_HEADS * A_D
Z_IQ = Z_BQ + B_HEADS * B_HD
Z_SM = Z_IQ + IDX_HEADS * IDX_DIM
SM_W = 2 * LANES
Z_W = Z_SM + SM_W
assert 2 * B_HD + IDX_DIM + IDX_HEADS <= SM_W


def _cparams(sem):
    return pltpu.CompilerParams(dimension_semantics=sem, vmem_limit_bytes=VMEM_LIMIT)


def _rms(x, gain):
    ms = jnp.mean(x * x, axis=-1, keepdims=True)
    return x * lax.rsqrt(ms + EPS) * gain


def _sigmoid(x):
    return 1.0 / (1.0 + jnp.exp(-x))


GELU_C = float(np.sqrt(2.0 / np.pi))
GELU_C3 = GELU_C * 0.044715


def _gelu_tanh(x):
    half = 0.5 * x
    return half + half * jnp.tanh(x * (GELU_C + GELU_C3 * (x * x)))


def _dot(a, b):
    return jnp.dot(a, b, preferred_element_type=F32)


def _dot_nt(a, b):
    return lax.dot_general(a, b, (((1,), (1,)), ((), ())), preferred_element_type=F32)


def _lane_tile(x, reps):
    return jnp.concatenate([x] * reps, axis=1)


def _dot_tn(a, b):
    return lax.dot_general(a, b, (((0,), (0,)), ((), ())), preferred_element_type=F32)


def _norm_proj_kernel(x_ref, g_ref, w_ref, o_ref, xn_ref):
    @pl.when(pl.program_id(1) == 0)
    def _():
        xn_ref[...] = _rms(x_ref[...], g_ref[...]).astype(BF16)

    o_ref[...] = _dot(xn_ref[...], w_ref[...]).astype(o_ref.dtype)


def _norm_proj(x2d, gain, w, *, tm, tn, out_dtype):
    m, d = x2d.shape
    n = w.shape[1]
    return pl.pallas_call(
        _norm_proj_kernel,
        grid=(m // tm, n // tn),
        in_specs=[pl.BlockSpec((tm, d), lambda i, j: (i, 0)),
                  pl.BlockSpec((1, d), lambda i, j: (0, 0)),
                  pl.BlockSpec((d, tn), lambda i, j: (0, j))],
        out_specs=pl.BlockSpec((tm, tn), lambda i, j: (i, j)),
        out_shape=jax.ShapeDtypeStruct((m, n), out_dtype),
        scratch_shapes=[pltpu.VMEM((tm, d), BF16)],
        compiler_params=_cparams(("arbitrary", "arbitrary")),
        name="even_in_proj",
    )(x2d, gain.reshape(1, d), w)


def _hgrn_consts(c):
    levels = int(np.log2(c))
    tri = np.tril(np.ones((c, c), np.float32))
    masks = []
    t = np.arange(c)
    for l in range(levels):
        h = c >> (l + 1)
        base = (t // (2 * h)) * (2 * h)
        second = (t % (2 * h)) >= h
        same = base[:, None] == base[None, :]
        masks.append((same & second[:, None] & (~second)[None, :]).astype(np.float32))
    masks.append(np.eye(c, dtype=np.float32))
    return tri, np.stack(masks, 0), levels


def _hgrn_kernel(layer, c, levels, q_ref, f_ref, i_ref, g_ref, lbl_ref, an_ref, tri_ref, msk_ref,
                 o_ref, st_ref, b_ref):
    @pl.when(pl.program_id(1) == 0)
    def _():
        st_ref[...] = jnp.zeros_like(st_ref)

    lg = lbl_ref[...]
    e = jnp.exp(lg - jnp.max(lg, axis=0, keepdims=True))
    p = e / jnp.sum(e, axis=0, keepdims=True)
    lb = jnp.sum(p[:layer + 1], axis=0, keepdims=True)
    an = an_ref[...]
    n_chunks = q_ref.shape[1] // c
    width = A_HEADS * A_D
    row_pos = lax.broadcasted_iota(jnp.int32, (c, 1), 0)

    def chunk_row(bb, r0):
        qr = q_ref[bb, pl.ds(r0, c), :].astype(F32)
        fr = f_ref[bb, pl.ds(r0, c), :].astype(F32)
        v = i_ref[bb, pl.ds(r0, c), :].astype(BF16)
        gr = g_ref[bb, pl.ds(r0, c), :].astype(F32)
        q = qr * _sigmoid(qr)
        f = lb + (1.0 - lb) * _sigmoid(fr)
        g = jnp.log(f)
        k = 1.0 - f
        g1 = g.astype(BF16)
        r1 = g - g1.astype(F32)
        g2 = r1.astype(BF16)
        g3 = (r1 - g2.astype(F32)).astype(BF16)
        rs = _dot(tri_ref[...], jnp.concatenate([g1, g2, g3], axis=1))
        b = rs[:, :width] + rs[:, width:2 * width] + rs[:, 2 * width:]
        b_ref[bb] = b

        def midpoint(l):
            h = c >> (l + 1)
            if 2 * h >= SUBLANES:
                return jnp.concatenate([jnp.broadcast_to(b_ref[bb, base + h - 1:base + h, :], (2 * h, width))
                                        for base in range(0, c, 2 * h)], axis=0)
            out = b
            for k_ in range(2 * h):
                if k_ != h - 1:
                    out = jnp.where(row_pos % (2 * h) == k_, pltpu.roll(b, (k_ - h + 1) % c, axis=0), out)
            return out

        hs = [slice(h * A_D, (h + 1) * A_D) for h in range(A_HEADS)]
        qb_, kb_ = q.astype(BF16), k.astype(BF16)
        scores = [msk_ref[levels] * _dot_nt(qb_[:, s_], kb_[:, s_]) for s_ in hs]
        for l in range(levels):
            x = jnp.exp(-jnp.abs(b - midpoint(l)))
            qx, kx = (q * x).astype(BF16), (k * x).astype(BF16)
            scores = [sc + msk_ref[l] * _dot_nt(qx[:, s_], kx[:, s_]) for sc, s_ in zip(scores, hs)]
        qe = (q * jnp.exp(b)).astype(BF16)
        b_last = b[c - 1:c, :]
        kd = (k * jnp.exp(b_last - b)).astype(BF16)
        decay = jnp.exp(b_last)
        gate = gr * _sigmoid(gr)
        for h, s_ in enumerate(hs):
            st = st_ref[bb, h]
            o = _dot_nt(qe[:, s_], st.astype(BF16)) + _dot(scores[h].astype(BF16), v[:, s_])
            st_ref[bb, h] = decay[:, s_] * st + _dot_tn(v[:, s_], kd[:, s_])
            o_ref[bb, pl.ds(r0, c), s_] = (_rms(o, an[:, s_]) * gate[:, s_]).astype(o_ref.dtype)

    def chunk(ci, carry):
        r0 = pl.multiple_of(ci * c, c)
        for bb in range(q_ref.shape[0]):
            chunk_row(bb, r0)
        return carry

    lax.fori_loop(0, n_chunks, chunk, 0)


def _hgrn(z3, lb_logits, a_norm, layer, *, t_rows, b_rows):
    bsz, s, _ = z3.shape
    tri, msk, levels = _hgrn_consts(CHUNK)
    tri = jnp.asarray(tri, BF16)
    msk = jnp.asarray(msk, F32)
    nl = lb_logits.shape[0]
    width = A_HEADS * A_D
    col = Z_A // width

    def zspec(k):
        return pl.BlockSpec((b_rows, t_rows, width), lambda b, t: (b, t, col + k))

    return pl.pallas_call(
        functools.partial(_hgrn_kernel, layer, CHUNK, levels),
        grid=(bsz // b_rows, s // t_rows),
        in_specs=[zspec(0), zspec(1), zspec(2), zspec(3),
                  pl.BlockSpec((nl, width), lambda b, t: (0, 0)),
                  pl.BlockSpec((1, width), lambda b, t: (0, 0)),
                  pl.BlockSpec(tri.shape, lambda b, t: (0, 0)),
                  pl.BlockSpec(msk.shape, lambda b, t: (0, 0, 0))],
        out_specs=pl.BlockSpec((b_rows, t_rows, width), lambda b, t: (b, t, 0)),
        out_shape=jax.ShapeDtypeStruct((bsz, s, width), BF16),
        scratch_shapes=[pltpu.VMEM((b_rows, A_HEADS, A_D, A_D), F32),
                        pltpu.VMEM((b_rows, CHUNK, width), F32)],
        compiler_params=_cparams(("arbitrary", "arbitrary")),
        name="hgrn2",
    )(z3, z3, z3, z3, lb_logits, a_norm.reshape(1, -1), tri, msk)


INT_MIN = -2 ** 31
KEY_MASK = 0x7FFFFFFF
HALF_BITS = 16
HALF_MASK = (1 << HALF_BITS) - 1
HALF_MIN = -(1 << (HALF_BITS - 1))
HALF_MAX = (1 << (HALF_BITS - 1)) - 1
MASKED = -1e30
SHIFT_SAFE = 30.0
ONE_COL = B_HD
COUNT_CHAINS = 4

def _sort_key(x):
    bits = pltpu.bitcast(x, jnp.int32)
    key = bits ^ ((bits >> 31) & KEY_MASK)
    return jnp.where(x == 0.0, 0, key)


def _head_col(h, q):
    return ((h % 2) * (B_HEADS // 2) + h // 2) * q


def _dsa_kernel(topk, tk, zq_ref, zi_ref, zs_ref, qg_ref, kg_ref, bd_ref, low_ref, o_ref,
                kn_ref, ki_ref, va_ref, vb_ref, kmax_ref, key_ref, khi_ref, klo_ref, qat_ref, qit_ref, p_ref,
                acc_ref):
    qb = pl.program_id(1)
    q = zq_ref.shape[0]
    half = (B_HEADS // 2) * q
    n_tiles_all = kn_ref.shape[0]
    lane = lax.broadcasted_iota(jnp.int32, (1, LANES), 1)
    lo_half = lane < B_HD
    one_col = jnp.where(lane == ONE_COL, 1.0, 0.0)

    @pl.when(qb == 0)
    def _():
        kg = kg_ref[...]
        qat_ref[...] = jnp.zeros_like(qat_ref)
        qit_ref[...] = jnp.zeros_like(qit_ref)

        def prep(t, kmax):
            r0 = pl.multiple_of(t * tk, tk)
            rows = zs_ref[pl.ds(r0, tk), :].astype(F32)
            kv = rows[:, :LANES]
            ms = jnp.sum(jnp.where(lo_half, kv * kv, 0.0), axis=-1, keepdims=True) * (1.0 / B_HD)
            kn = jnp.where(lo_half, kv * lax.rsqrt(ms + EPS) * kg, 0.0)
            kn_ref[t] = (kn + one_col).astype(BF16)
            ki_ref[t] = jnp.where(lo_half, rows[:, LANES:], 0.0).astype(BF16)
            va_ref[t] = jnp.where(lo_half, pltpu.roll(kv, B_HD, axis=1), 1.0).astype(BF16)
            vb_ref[t] = jnp.where(lo_half, 1.0, kv).astype(BF16)
            return jnp.maximum(kmax, jnp.max(jnp.sum(kn * kn, axis=-1, keepdims=True), axis=0, keepdims=True))

        kmax = lax.fori_loop(0, n_tiles_all, prep, jnp.zeros((1, 1), F32))
        kmax_ref[...] = jnp.broadcast_to(jnp.sqrt(kmax), kmax_ref.shape)

    qf = zq_ref[...].astype(F32)
    sq = qf * qf
    sq_hi = sq.astype(BF16)
    sq_lo = (sq - sq_hi.astype(F32)).astype(BF16)
    ssum = _dot(sq_hi, bd_ref[...]) + _dot(sq_lo, bd_ref[...])
    qnt = (qf * lax.rsqrt(ssum * (1.0 / B_HD) + EPS) * qg_ref[...]).T
    qit = zi_ref[...].astype(F32).T
    kmax_q = _lane_tile(kmax_ref[0:1, :], q // LANES)
    bound = []
    for h in range(B_HEADS):
        c0 = _head_col(h, q)
        blk = qnt[h * B_HD:(h + 1) * B_HD, :]
        qat_ref[0:B_HD, c0:c0 + q] = blk.astype(BF16)
        qit_ref[0:IDX_DIM, c0:c0 + q] = qit[h * IDX_DIM:(h + 1) * IDX_DIM, :].astype(BF16)
        bound.append(jnp.sqrt(jnp.sum(blk * blk, axis=0, keepdims=True)) * kmax_q)
    q0 = pl.multiple_of(qb * q, q)
    wt = zs_ref[pl.ds(q0, q), :].astype(F32)[:, LANES:].T
    wt = wt[IDX_DIM:IDX_DIM + IDX_HEADS, :] * ((IDX_HEADS ** -0.5) * (IDX_DIM ** -0.5))

    n_tiles = (q0 + q + tk - 1) // tk
    qpos = lax.broadcasted_iota(jnp.int32, (1, q), 1)
    limit = q0 + ((qpos // CHUNK) + 1) * CHUNK

    def index_tile(t, carry):
        ki = ki_ref[t]
        acc = jnp.zeros((tk, q), F32)
        for h in range(IDX_HEADS):
            c0 = _head_col(h, q)
            acc = acc + jnp.maximum(_dot(ki, qit_ref[:, c0:c0 + q]), 0.0) * wt[h:h + 1, :]
        kpos = t * tk + lax.broadcasted_iota(jnp.int32, (tk, 1), 0)
        key = jnp.where(kpos < limit, _sort_key(acc), INT_MIN)
        key_ref[t] = key
        khi_ref[t] = (key >> HALF_BITS).astype(jnp.int16)
        klo_ref[t] = ((key & HALF_MASK) + HALF_MIN).astype(jnp.int16)
        return carry

    lax.fori_loop(0, n_tiles, index_tile, 0)

    def count_ge(cand):
        def body(t, acc):
            hit = jnp.where(key_ref[t] >= cand, 1.0, 0.0)
            return acc + jnp.sum(hit.reshape(COUNT_CHAINS, tk // (COUNT_CHAINS * SUBLANES), SUBLANES, q), axis=1)
        acc = lax.fori_loop(0, n_tiles, body, jnp.zeros((COUNT_CHAINS, SUBLANES, q), F32))
        return jnp.sum(jnp.sum(acc, axis=0), axis=0, keepdims=True)

    one16 = jnp.ones((), jnp.int16)
    packed_rows = 2 * SUBLANES

    def count_ge16(cand):
        c16 = cand.astype(jnp.int16)

        def body(t, acc):
            x = jnp.where(khi_ref[t] >= c16, one16, 0 * one16).reshape(tk // packed_rows, packed_rows, q)
            while x.shape[0] > 1:
                x = x[:x.shape[0] // 2] + x[x.shape[0] // 2:]
            return acc + x[0]
        acc = lax.fori_loop(0, n_tiles, body, jnp.zeros((packed_rows, q), jnp.int16))
        return jnp.sum(acc.astype(jnp.int32), axis=0, keepdims=True)

    def half_search():
        def bit_step(i, top):
            cand = top + lax.shift_left(jnp.int32(1), HALF_BITS - 1 - i)
            return jnp.where(count_ge16(cand) >= topk, cand, top)
        return lax.fori_loop(0, HALF_BITS, bit_step, jnp.full((1, q), HALF_MIN, jnp.int32))

    top = half_search()
    top16 = top.astype(jnp.int16)

    def narrow(t, carry):
        hi = khi_ref[t]
        khi_ref[t] = jnp.where(hi > top16, HALF_MAX * one16, jnp.where(hi < top16, HALF_MIN * one16, klo_ref[t]))
        return carry

    lax.fori_loop(0, n_tiles, narrow, 0)
    thr = top * (1 << HALF_BITS) + (half_search() - HALF_MIN)
    need = float(topk) - count_ge(thr + 1)
    need = jnp.where(thr == INT_MIN, 0.0, need)

    def bias_tile(t, run):
        keys = key_ref[t]
        eq = keys == thr
        eqf = jnp.where(eq, 1.0, 0.0)
        rank = _dot(low_ref[...], eqf.astype(BF16)) + run
        tie_ok = jnp.where(eq, rank, float(topk)) < need
        bias = jnp.where(keys > thr, 0.0, jnp.where(tie_ok, 0.0, MASKED))
        return bias, run + jnp.sum(eqf, axis=0, keepdims=True)

    run0 = jnp.zeros((1, q), F32)
    bound = jnp.concatenate([bound[h] for h in sorted(range(B_HEADS), key=lambda h: _head_col(h, q))], axis=1)

    def exact_shift():
        def body(t, carry):
            run, mx = carry
            bias, run = bias_tile(t, run)
            s = _dot(kn_ref[t], qat_ref[...])
            cols = [jnp.max(s[:, c0:c0 + q] + bias, axis=0, keepdims=True) for c0 in range(0, 2 * half, q)]
            return run, jnp.maximum(mx, jnp.concatenate(cols, axis=1))
        return lax.fori_loop(0, n_tiles, body, (run0, jnp.full((1, 2 * half), MASKED, F32)))[1]

    shift_rows = lax.broadcasted_iota(jnp.int32, (2 * SUBLANES, 1), 0) == 0
    qat_ref[B_HD:B_HD + 2 * SUBLANES, :] = jnp.zeros((2 * SUBLANES, 2 * half), BF16)
    shift = lax.cond(jnp.max(bound) > SHIFT_SAFE, exact_shift, lambda: bound)
    qat_ref[B_HD:B_HD + 2 * SUBLANES, :] = jnp.where(shift_rows, -shift, 0.0).astype(BF16)

    acc_ref[...] = jnp.zeros_like(acc_ref)

    def attend(t, run):
        bias, run = bias_tile(t, run)
        kn = kn_ref[t]
        for c0 in range(0, 2 * half, q):
            p_ref[:, c0:c0 + q] = jnp.exp(_dot(kn, qat_ref[:, c0:c0 + q]) + bias).astype(BF16)
        acc_ref[:half] += _dot_tn(p_ref[:, :half], va_ref[t])
        acc_ref[half:] += _dot_tn(p_ref[:, half:], vb_ref[t])
        return run

    lax.fori_loop(0, n_tiles, attend, run0)
    for j in range(B_HEADS // 2):
        a = acc_ref[j * q:(j + 1) * q]
        b = acc_ref[half + j * q:half + (j + 1) * q]
        o = jnp.where(lo_half, a / pltpu.roll(a, B_HD, axis=1), b / pltpu.roll(b, B_HD, axis=1))
        o_ref[:, j * LANES:(j + 1) * LANES] = o.astype(o_ref.dtype)


def _dsa(z3, q_gain, k_gain, *, tk, qrows):
    bsz, s, _ = z3.shape
    topk = min(TOPK_MAX, s // 4)
    assert tk >= topk >= 2 and s % tk == 0 and tk % LANES == 0 and s % qrows == 0 and qrows % LANES == 0
    groups = tk // (2 * SUBLANES)
    assert groups & (groups - 1) == 0 and s // (2 * SUBLANES) <= HALF_MAX
    n_t = s // tk
    w_q = B_HEADS * B_HD
    low = jnp.asarray(np.tril(np.ones((tk, tk), np.float32), -1), BF16)
    head_of = np.arange(w_q) // B_HD
    bd = jnp.asarray(head_of[:, None] == head_of[None, :], BF16)
    kg = jnp.concatenate([k_gain.astype(F32), jnp.ones((LANES - B_HD,), F32)]).reshape(1, LANES)
    qg = jnp.tile(q_gain.astype(F32) * (B_HD ** -0.5), B_HEADS).reshape(1, w_q)

    def const(a):
        return pl.BlockSpec(a.shape, lambda b, i: (0,) * a.ndim)

    return pl.pallas_call(
        functools.partial(_dsa_kernel, topk, tk),
        grid=(bsz, s // qrows),
        in_specs=[pl.BlockSpec((None, qrows, w_q), lambda b, i: (b, i, Z_BQ // w_q)),
                  pl.BlockSpec((None, qrows, w_q), lambda b, i: (b, i, Z_IQ // w_q)),
                  pl.BlockSpec((None, s, SM_W), lambda b, i: (b, 0, Z_SM // SM_W)),
                  const(qg), const(kg), const(bd), const(low)],
        out_specs=pl.BlockSpec((None, qrows, w_q), lambda b, i: (b, i, 0)),
        out_shape=jax.ShapeDtypeStruct((bsz, s, w_q), BF16),
        scratch_shapes=[pltpu.VMEM((n_t, tk, LANES), BF16),
                        pltpu.VMEM((n_t, tk, LANES), BF16),
                        pltpu.VMEM((n_t, tk, LANES), BF16),
                        pltpu.VMEM((n_t, tk, LANES), BF16),
                        pltpu.VMEM((SUBLANES, LANES), F32),
                        pltpu.VMEM((n_t, tk, qrows), jnp.int32),
                        pltpu.VMEM((n_t, tk, qrows), jnp.int16),
                        pltpu.VMEM((n_t, tk, qrows), jnp.int16),
                        pltpu.VMEM((LANES, B_HEADS * qrows), BF16),
                        pltpu.VMEM((LANES, IDX_HEADS * qrows), BF16),
                        pltpu.VMEM((tk, B_HEADS * qrows), BF16),
                        pltpu.VMEM((B_HEADS * qrows, LANES), F32)],
        compiler_params=_cparams(("arbitrary", "arbitrary")),
        name="dsa",
    )(z3, z3, z3, qg, kg, bd, low)


def _causal_conv(u, prev, w_ref, b):
    width = w_ref.shape[0]
    rows = u.shape[0]

    def taps(x):
        y = w_ref[width - 1:width, :] * x + b
        for i in range(width - 1):
            y = y + w_ref[i:i + 1, :] * pltpu.roll(x, width - 1 - i, axis=0)
        return y

    head = taps(jnp.concatenate([prev, u[:SUBLANES]], axis=0))[SUBLANES:]
    return jnp.concatenate([head, taps(u)[SUBLANES:]], axis=0) if rows > SUBLANES else head


def _ffn_kernel(mixed, x_ref, g_ref, wg_ref, wv_ref, cwg_ref, cwv_ref, cbg_ref, cbv_ref, wd_ref, *rest):
    if mixed:
        ma_ref, mb_ref, wm_ref, o_ref, xn_ref, pg_ref, pv_ref = rest
    else:
        o_ref, xn_ref, pg_ref, pv_ref = rest
    s = pl.program_id(1)
    j = pl.program_id(2)
    rows = x_ref.shape[0]

    @pl.when(j == 0)
    def _():
        h = x_ref[...]
        if mixed:
            ka = ma_ref.shape[1]
            h = h + _dot(ma_ref[...], wm_ref[:ka, :]) + _dot(mb_ref[...], wm_ref[ka:, :])
        xn_ref[...] = _rms(h, g_ref[...]).astype(BF16)
        o_ref[...] = h

    @pl.when(s == 0)
    def _():
        pg_ref[j] = jnp.zeros(pg_ref.shape[1:], F32)
        pv_ref[j] = jnp.zeros(pv_ref.shape[1:], F32)

    xn = xn_ref[...]

    def branch(w_ref, cw_ref, cb_ref, prev_ref):
        u = _dot(xn, w_ref[...])
        prev = prev_ref[j]
        prev_ref[j] = u[rows - SUBLANES:]
        return _causal_conv(u, prev, cw_ref, cb_ref[...])

    gate = branch(wg_ref, cwg_ref, cbg_ref, pg_ref)
    val = branch(wv_ref, cwv_ref, cbv_ref, pv_ref)
    act = _gelu_tanh(gate) * val
    o_ref[...] += _dot(act.astype(BF16), wd_ref[...])


def _conv_ffn(h3, gain, w_up, conv_w, conv_b, w_down, *, ts, tf, mix=None):
    bsz, s, d = h3.shape
    dff = w_down.shape[0]
    nj = dff // tf
    width = conv_w.shape[0]
    cb = conv_b.reshape(1, -1)
    xspec = pl.BlockSpec((None, ts, d), lambda b, i, j: (b, i, 0))
    mix_args, mix_specs = (), []
    if mix is not None:
        ma, mb, wm = mix
        mix_args = (ma, mb, wm)
        mix_specs = [pl.BlockSpec((None, ts, ma.shape[-1]), lambda b, i, j: (b, i, 0)),
                     pl.BlockSpec((None, ts, mb.shape[-1]), lambda b, i, j: (b, i, 0)),
                     pl.BlockSpec(wm.shape, lambda b, i, j: (0, 0))]
    return pl.pallas_call(
        functools.partial(_ffn_kernel, mix is not None),
        grid=(bsz, s // ts, nj),
        in_specs=[xspec,
                  pl.BlockSpec((1, d), lambda b, i, j: (0, 0)),
                  pl.BlockSpec((d, tf), lambda b, i, j: (0, j)),
                  pl.BlockSpec((d, tf), lambda b, i, j: (0, j + nj)),
                  pl.BlockSpec((width, tf), lambda b, i, j: (0, j)),
                  pl.BlockSpec((width, tf), lambda b, i, j: (0, j + nj)),
                  pl.BlockSpec((1, tf), lambda b, i, j: (0, j)),
                  pl.BlockSpec((1, tf), lambda b, i, j: (0, j + nj)),
                  pl.BlockSpec((tf, d), lambda b, i, j: (j, 0))] + mix_specs,
        out_specs=xspec,
        out_shape=jax.ShapeDtypeStruct((bsz, s, d), F32),
        scratch_shapes=[pltpu.VMEM((ts, d), BF16),
                        pltpu.VMEM((nj, SUBLANES, tf), F32),
                        pltpu.VMEM((nj, SUBLANES, tf), F32)],
        compiler_params=_cparams(("arbitrary", "arbitrary", "arbitrary")),
        name="conv_ffn",
    )(h3, gain.reshape(1, d), w_up, w_up, conv_w, conv_w, cb, cb, w_down, *mix_args)


def _rglru_kernel(x_ref, g_ref, win_ref, cw_ref, cb_ref, wg_ref, ba_ref, bx_ref, lam_ref,
                  wout_ref, o_ref, a_ref, u_ref, y_ref, px_ref, ph_ref):
    rows = x_ref.shape[0]
    width = LRU_BLOCKS * LRU_BW

    @pl.when(pl.program_id(1) == 0)
    def _():
        px_ref[...] = jnp.zeros_like(px_ref)
        ph_ref[...] = jnp.zeros_like(ph_ref)

    xn = _rms(x_ref[...], g_ref[...]).astype(BF16)
    lam = lam_ref[...]
    log_sig = jnp.minimum(lam, 0.0) - jnp.log1p(jnp.exp(-jnp.abs(lam)))
    pair = 2 * LRU_BW
    for n2 in range(LRU_BLOCKS // 2):
        cs2 = slice(n2 * pair, (n2 + 1) * pair)
        y_ref[:, cs2] = _gelu_tanh(_dot(xn, win_ref[:, cs2]))
        xb = _dot(xn, win_ref[:, width + n2 * pair:width + (n2 + 1) * pair])
        prev = px_ref[:, cs2]
        px_ref[:, cs2] = xb[rows - SUBLANES:]
        xc2 = _causal_conv(xb, prev, cw_ref.at[:, cs2], cb_ref[:, cs2])
        for n in (2 * n2, 2 * n2 + 1):
            cs = slice(n * LRU_BW, (n + 1) * LRU_BW)
            xc = xc2[:, (n - 2 * n2) * LRU_BW:(n - 2 * n2 + 1) * LRU_BW]
            rg = _dot(xc.astype(BF16), wg_ref[n])
            r = _sigmoid(rg[:, :LRU_BW] + ba_ref[:, cs])
            gi = _sigmoid(rg[:, LRU_BW:] + bx_ref[:, cs])
            log_a = LRU_C * r * log_sig[:, cs]
            a = jnp.exp(log_a)
            a_ref[:, cs] = a
            u_ref[:, cs] = jnp.sqrt(-jnp.tanh(log_a) * (a * a + 1.0)) * (gi * xc)

    sub = lax.broadcasted_iota(jnp.int32, (SUBLANES, 1), 0)

    def group(gi_, h):
        r0 = pl.multiple_of(gi_ * SUBLANES, SUBLANES)
        a = a_ref[pl.ds(r0, SUBLANES), :]
        b = u_ref[pl.ds(r0, SUBLANES), :]
        d = 1
        while d < SUBLANES:
            keep = sub >= d
            b = jnp.where(keep, a * pltpu.roll(b, d, axis=0) + b, b)
            a = jnp.where(keep, a * pltpu.roll(a, d, axis=0), a)
            d *= 2
        hs = a * h + b
        u_ref[pl.ds(r0, SUBLANES), :] = hs
        return hs[SUBLANES - 1:, :]

    ph_ref[0:1, :] = lax.fori_loop(0, rows // SUBLANES, group, ph_ref[0:1, :], unroll=2)
    o_ref[...] = x_ref[...] + _dot((u_ref[...] * y_ref[...]).astype(BF16), wout_ref[...])


def _rglru(h3, gain, w_in, conv_w, conv_b, wa, ba, wx, bx, lam, w_out, *, ts):
    bsz, s, d = h3.shape
    width = LRU_BLOCKS * LRU_BW
    xspec = pl.BlockSpec((None, ts, d), lambda b, i: (b, i, 0))

    def full(a):
        return pl.BlockSpec(a.shape, lambda b, i: (0,) * a.ndim)

    wg = jnp.concatenate([wa, wx], axis=-1)
    args = (gain.reshape(1, d), w_in, conv_w, conv_b.reshape(1, width), wg, ba.reshape(1, width),
            bx.reshape(1, width), lam.reshape(1, width), w_out)
    return pl.pallas_call(
        _rglru_kernel,
        grid=(bsz, s // ts),
        in_specs=[xspec] + [full(a) for a in args],
        out_specs=xspec,
        out_shape=jax.ShapeDtypeStruct((bsz, s, d), F32),
        scratch_shapes=[pltpu.VMEM((ts, width), F32),
                        pltpu.VMEM((ts, width), F32),
                        pltpu.VMEM((ts, width), F32),
                        pltpu.VMEM((SUBLANES, width), F32),
                        pltpu.VMEM((SUBLANES, width), F32)],
        compiler_params=_cparams(("arbitrary", "arbitrary")),
        name="rglru_block",
    )(h3, *args)


def _tile(n, pref):
    t = min(n, pref)
    assert n % t == 0
    return t


def _tiles(bsz, s):
    return dict(
        proj_rows=_tile(bsz * s, 512),
        hgrn_rows=_tile(s, 512),
        hgrn_batch=_tile(bsz, 2),
        dsa_keys=_tile(s, 512),
        dsa_queries=_tile(s, 256),
        lru_rows=_tile(s, 512),
        ffn_rows=_tile(s, 1024),
        ffn_cols=1024,
    )


def kernel(x, lb_logits, even_norm, even_w_in, even_w_out, a_out_norm, b_q_norm, b_k_norm, odd_norm, odd_w_in, odd_conv_w, odd_conv_b, odd_gate_a_w, odd_gate_a_b, odd_gate_x_w, odd_gate_x_b, odd_lambda, odd_w_out, ffn_norm, ffn_w_up, ffn_conv_w, ffn_conv_b, ffn_w_down):
    bsz, s, d = x.shape
    depth = ffn_norm.shape[0]
    t = _tiles(bsz, s)
    h = x
    for layer in range(depth):
        j = layer // 2
        if layer % 2 == 0:
            w = even_w_in[j]
            o_bk = Z_IQ
            o_iq = o_bk + 2 * B_HD
            o_ik = o_iq + IDX_HEADS * IDX_DIM
            o_end = o_ik + IDX_DIM + IDX_HEADS
            assert w.shape[1] == o_end
            w = jnp.concatenate(
                [w[:, :o_bk], w[:, o_iq:o_ik], w[:, o_bk:o_iq], w[:, o_ik:o_end],
                 jnp.zeros((d, Z_W - o_end), w.dtype)], axis=1).astype(BF16)
            z = _norm_proj(h.reshape(bsz * s, d), even_norm[j], w,
                           tm=t["proj_rows"], tn=Z_W, out_dtype=BF16)
            z3 = z.reshape(bsz, s, Z_W)
            a_o = _hgrn(z3, lb_logits, a_out_norm[j], layer, t_rows=t["hgrn_rows"], b_rows=t["hgrn_batch"])
            b_o = _dsa(z3, b_q_norm[j], b_k_norm[j], tk=t["dsa_keys"], qrows=t["dsa_queries"])
            mix = (a_o, b_o, even_w_out[j].astype(BF16))
        else:
            mix = None
            h = _rglru(h, odd_norm[j], odd_w_in[j].astype(BF16), odd_conv_w[j], odd_conv_b[j],
                       odd_gate_a_w[j].astype(BF16), odd_gate_a_b[j], odd_gate_x_w[j].astype(BF16),
                       odd_gate_x_b[j], odd_lambda[j], odd_w_out[j].astype(BF16), ts=t["lru_rows"])
        h = _conv_ffn(h, ffn_norm[layer], ffn_w_up[layer].astype(BF16), ffn_conv_w[layer],
                      ffn_conv_b[layer], ffn_w_down[layer].astype(BF16), ts=t["ffn_rows"], tf=t["ffn_cols"],
                      mix=mix)
    return h
```
